```python
import math
import jax, jax.numpy as jnp
from jax import lax
import numpy as np

D_MODEL = 2048
BATCH = 8
SEQ = 2048
DEPTH = 1
DEC_BATCH = 4
DEC_SEQ = 2048
PAST_LEN = 128

GRID_W = 64
HEAD_DIM = 128
NA_HEADS = 8
DN_HEADS = 8
NA_WIDTH = NA_HEADS * HEAD_DIM
DN_WIDTH = DN_HEADS * HEAD_DIM
MIX_WIDTH = NA_WIDTH + DN_WIDTH
NA_ROWS_MAX = 8
NA_COLS = 16
CONV_K = 5
CHUNK = 64
D_FF = -(-8 * D_MODEL // (3 * 256)) * 256
RMS_EPS = 1e-6
L2_EPS = 1e-6

OFF_NA = 0
OFF_DN = OFF_NA + 3 * NA_WIDTH
OFF_Z = OFF_DN + 3 * DN_WIDTH
OFF_B = OFF_Z + DN_WIDTH
OFF_A = OFF_B + 2 * DN_HEADS
IN_COLS = OFF_A + 2 * DN_HEADS

kernel_name = "hybrid_na_gdn_encoder"


def rmsnorm(x, w):
    x32 = x.astype(jnp.float32)
    y = x32 * lax.rsqrt(jnp.mean(x32 * x32, axis=-1, keepdims=True) + RMS_EPS)
    return (y * w.astype(jnp.float32)).astype(x.dtype)


def l2norm(x):
    return x * lax.rsqrt(jnp.sum(x * x, axis=-1, keepdims=True) + L2_EPS)


def neighbourhood_attention(q, k, v, rpb):
    B, L, H, d = q.shape
    rows = L // GRID_W
    kr = min(NA_ROWS_MAX, rows)
    grid = lambda t: t.reshape(B, rows, GRID_W, H, d)
    qg, kg, vg = grid(q * (d ** -0.5)), grid(k), grid(v)
    cols = jnp.arange(GRID_W)
    col_start = jnp.clip(cols - NA_COLS // 2, 0, GRID_W - NA_COLS)
    col_idx = col_start[:, None] + jnp.arange(NA_COLS)[None, :]
    dc_idx = col_idx - cols[:, None] + (NA_COLS - 1)

    def row_block(r):
        rs = jnp.clip(r - kr // 2, 0, rows - kr)
        q_r = lax.dynamic_index_in_dim(qg, r, axis=1, keepdims=False)
        k_band = lax.dynamic_slice_in_dim(kg, rs, kr, axis=1)
        v_band = lax.dynamic_slice_in_dim(vg, rs, kr, axis=1)
        k_win = k_band[:, :, col_idx]
        v_win = v_band[:, :, col_idx]
        dr_idx = rs + jnp.arange(kr) - r + (NA_ROWS_MAX - 1)
        bias = rpb[:, dr_idx[None, :, None], dc_idx[:, None, :]]
        s = jnp.einsum('bchd,brcwhd->bhcrw', q_r, k_win).astype(jnp.float32) + bias.astype(jnp.float32)
        p = jax.nn.softmax(s.reshape(B, H, GRID_W, kr * NA_COLS), axis=-1)
        p = p.reshape(s.shape).astype(v.dtype)
        return jnp.einsum('bhcrw,brcwhd->bchd', p, v_win)

    out = lax.map(row_block, jnp.arange(rows))
    return out.transpose(1, 0, 2, 3, 4).reshape(B, L, H * d)


def centred_depthwise_conv(x, w):
    C = x.shape[-1]
    return lax.conv_general_dilated(
        x, w[:, None, :].astype(x.dtype), window_strides=(1,),
        padding=[(CONV_K // 2, CONV_K // 2)],
        dimension_numbers=('NWC', 'WIO', 'NWC'), feature_group_count=C)


def chunk_gated_delta_rule(q, k, v, g, beta):
    B, L, H, dk = q.shape
    n = L // CHUNK
    chunks = lambda t: t.reshape(B, n, CHUNK, H, -1).transpose(0, 3, 1, 2, 4)
    q = chunks(q * (dk ** -0.5))
    k, v = chunks(k), chunks(v)
    g = jnp.cumsum(g.reshape(B, n, CHUNK, H).transpose(0, 3, 1, 2), axis=-1)
    beta = beta.reshape(B, n, CHUNK, H).transpose(0, 3, 1, 2)
    tri = jnp.tril(jnp.ones((CHUNK, CHUNK), dtype=bool))
    strict = jnp.tril(jnp.ones((CHUNK, CHUNK), dtype=bool), k=-1)
    diff = g[..., :, None] - g[..., None, :]
    decay = jnp.where(tri, jnp.exp(jnp.where(tri, diff, 0.0)), 0.0)
    k_beta = k * beta[..., None]
    v_beta = v * beta[..., None]
    a_mat = jnp.where(strict, jnp.einsum('bhncd,bhnsd->bhncs', k_beta, k) * decay, 0.0)
    eye = jnp.eye(CHUNK, dtype=jnp.float32)
    t_mat = lax.linalg.triangular_solve(eye + a_mat, jnp.broadcast_to(eye, a_mat.shape),
                                        left_side=True, lower=True, unit_diagonal=True)
    u = jnp.einsum('bhncs,bhnse->bhnce', t_mat, v_beta)
    w = jnp.einsum('bhncs,bhnsd->bhncd', t_mat, k_beta * jnp.exp(g)[..., None])
    qk = jnp.where(tri, jnp.einsum('bhncd,bhnsd->bhncs', q, k) * decay, 0.0)

    def step(S, inp):
        q_i, k_i, u_i, w_i, g_i, qk_i = inp
        v_new = u_i - jnp.einsum('bhcd,bhde->bhce', w_i, S)
        o = jnp.einsum('bhcd,bhde->bhce', q_i * jnp.exp(g_i)[..., None], S) \
            + jnp.einsum('bhcs,bhse->bhce', qk_i, v_new)
        g_last = g_i[..., -1]
        S = S * jnp.exp(g_last)[..., None, None] + jnp.einsum(
            'bhcd,bhce->bhde', k_i * jnp.exp(g_last[..., None] - g_i)[..., None], v_new)
        return S, o

    to_scan = lambda t: jnp.moveaxis(t, 2, 0)
    xs = (to_scan(q), to_scan(k), to_scan(u), to_scan(w), to_scan(g), to_scan(qk))
    S0 = jnp.zeros((B, H, dk, v.shape[-1]), jnp.float32)
    _, o = lax.scan(step, S0, xs)
    return o.transpose(1, 0, 3, 2, 4).reshape(B, L, H, -1)


def gated_deltanet_bidir(qkv, z, b, a, conv_w, A_log, dt_bias, norm_w):
    B, L, _ = qkv.shape
    qkv = jax.nn.silu(centred_depthwise_conv(qkv, conv_w).astype(jnp.float32))
    q, k, v = jnp.split(qkv, 3, axis=-1)
    heads = lambda t: t.reshape(B, L, DN_HEADS, HEAD_DIM)
    q, k, v = l2norm(heads(q)), l2norm(heads(k)), heads(v)
    beta = jax.nn.sigmoid(b.astype(jnp.float32)).reshape(B, L, 2, DN_HEADS)
    g = -jnp.exp(A_log.astype(jnp.float32)) * jax.nn.softplus(
        a.astype(jnp.float32).reshape(B, L, 2, DN_HEADS) + dt_bias.astype(jnp.float32))
    flip = lambda t: jnp.flip(t, axis=1)
    o_fwd = chunk_gated_delta_rule(q, k, v, g[:, :, 0], beta[:, :, 0])
    o_bwd = flip(chunk_gated_delta_rule(flip(q), flip(k), flip(v), flip(g[:, :, 1]), flip(beta[:, :, 1])))
    o = o_fwd + o_bwd
    o = o * lax.rsqrt(jnp.mean(o * o, axis=-1, keepdims=True) + RMS_EPS) * norm_w.astype(jnp.float32)
    o = o * jax.nn.silu(heads(z).astype(jnp.float32))
    return o.reshape(B, L, DN_WIDTH)


def encoder_layer(h, norm_mix_w, w_in, na_rpb, dn_conv_w, dn_A_log, dn_dt_bias, dn_norm_w,
                  w_out, norm_ffn_w, w_gate, w_up, w_down):
    B, L, _ = h.shape
    xn = rmsnorm(h, norm_mix_w)
    proj = xn @ w_in
    na = lambda i: proj[..., OFF_NA + i * NA_WIDTH: OFF_NA + (i + 1) * NA_WIDTH].reshape(
        B, L, NA_HEADS, HEAD_DIM)
    y_na = neighbourhood_attention(na(0), na(1), na(2), na_rpb)
    y_dn = gated_deltanet_bidir(proj[..., OFF_DN:OFF_Z], proj[..., OFF_Z:OFF_B],
                                proj[..., OFF_B:OFF_A], proj[..., OFF_A:IN_COLS],
                                dn_conv_w, dn_A_log, dn_dt_bias, dn_norm_w)
    mixed = jnp.concatenate([y_na, y_dn.astype(h.dtype)], axis=-1) @ w_out
    h = h + mixed
    hn = rmsnorm(h, norm_ffn_w)
    return h + (jax.nn.silu(hn @ w_gate) * (hn @ w_up)) @ w_down


def trunk(x, norm_mix_w, w_in, na_rpb, dn_conv_w, dn_A_log, dn_dt_bias, dn_norm_w,
          w_out, norm_ffn_w, w_gate, w_up, w_down, final_norm_w):
    h = x
    for i in range(DEPTH):
        h = encoder_layer(h, norm_mix_w[i], w_in[i], na_rpb[i], dn_conv_w[i], dn_A_log[i],
                          dn_dt_bias[i], dn_norm_w[i], w_out[i], norm_ffn_w[i],
                          w_gate[i], w_up[i], w_down[i])
    return rmsnorm(h, final_norm_w)


def setup_inputs(seed: int = 0) -> dict:
    key = jax.random.key(seed)
    ks = jax.random.split(key, 16)
    f32 = jnp.float32
    nrm = lambda k, shape, fan_in: jax.random.normal(k, shape, f32) * (fan_in ** -0.5)
    gain = lambda k, shape: 1.0 + 0.02 * jax.random.normal(k, shape, f32)
    dt = jnp.exp(jax.random.uniform(ks[7], (DEPTH, 2, DN_HEADS), f32,
                                    math.log(1e-3), math.log(1e-1)))
    return {
        "x_prompt": jax.random.normal(ks[0], (BATCH, SEQ, D_MODEL), f32),
        "x_sample": jax.random.normal(ks[1], (DEC_BATCH, DEC_SEQ, D_MODEL), f32),
        "norm_mix_w": gain(ks[2], (DEPTH, D_MODEL)),
        "w_in": nrm(ks[3], (DEPTH, D_MODEL, IN_COLS), D_MODEL),
        "na_rpb": 0.02 * jax.random.normal(ks[4], (DEPTH, NA_HEADS, 2 * NA_ROWS_MAX - 1, 2 * NA_COLS - 1), f32),
        "dn_conv_w": nrm(ks[5], (DEPTH, CONV_K, 3 * DN_WIDTH), CONV_K),
        "dn_A_log": jnp.log(jax.random.uniform(ks[6], (DEPTH, 2, DN_HEADS), f32, 1.0, 16.0)),
        "dn_dt_bias": dt + jnp.log(-jnp.expm1(-dt)),
        "dn_norm_w": gain(ks[8], (DEPTH, HEAD_DIM)),
        "w_out": nrm(ks[9], (DEPTH, MIX_WIDTH, D_MODEL), MIX_WIDTH),
        "norm_ffn_w": gain(ks[10], (DEPTH, D_MODEL)),
        "w_gate": nrm(ks[11], (DEPTH, D_MODEL, D_FF), D_MODEL),
        "w_up": nrm(ks[12], (DEPTH, D_MODEL, D_FF), D_MODEL),
        "w_down": nrm(ks[13], (DEPTH, D_FF, D_MODEL), D_FF),
        "final_norm_w": gain(ks[14], (D_MODEL,)),
    }


def reference(x_prompt, x_sample, norm_mix_w, w_in, na_rpb, dn_conv_w, dn_A_log, dn_dt_bias,
              dn_norm_w, w_out, norm_ffn_w, w_gate, w_up, w_down, final_norm_w):
    y_prompt = trunk(x_prompt, norm_mix_w, w_in, na_rpb, dn_conv_w, dn_A_log, dn_dt_bias,
                     dn_norm_w, w_out, norm_ffn_w, w_gate, w_up, w_down, final_norm_w)
    y_sample = trunk(x_sample, norm_mix_w, w_in, na_rpb, dn_conv_w, dn_A_log, dn_dt_bias,
                     dn_norm_w, w_out, norm_ffn_w, w_gate, w_up, w_down, final_norm_w)
    return (y_prompt, y_sample)
```

```python
import functools

import jax
import jax.numpy as jnp
from jax import lax
from jax.experimental import pallas as pl
from jax.experimental.pallas import tpu as pltpu

D_MODEL = 2048
GRID_W = 64
HEAD_DIM = 128
NA_HEADS = 8
DN_HEADS = 8
NA_WIDTH = NA_HEADS * HEAD_DIM
DN_WIDTH = DN_HEADS * HEAD_DIM
NA_ROWS = 8
NA_COLS = 16
CONV_K = 5
CHUNK = 64
D_FF = 5632
RMS_EPS = 1e-6
L2_EPS = 1e-6

OFF_NA = 0
OFF_DN = OFF_NA + 3 * NA_WIDTH
OFF_Z = OFF_DN + 3 * DN_WIDTH
OFF_B = OFF_Z + DN_WIDTH
OFF_A = OFF_B + 2 * DN_HEADS
MAIN_COLS = OFF_B

LANES = 128
NPAIR = 2 * DN_HEADS
G_BETA, G_GHI, G_GLO, G_EG, G_ED, G_EL = range(6)
MASK_NEG = -1e30
VMEM_LIMIT = 56 * 1024 * 1024

F32 = jnp.float32
BF16 = jnp.bfloat16


def _silu(x):
    return x * (1.0 / (1.0 + jnp.exp(-x)))


def _nt(a, b):
    return lax.dot_general(a, b, (((1,), (1,)), ((), ())), preferred_element_type=F32)


def _nn(a, b):
    return jnp.dot(a, b, preferred_element_type=F32)


def _in_proj_kernel(x_ref, nw_ref, w_ref, wba_ref, proj_ref, ba_ref, xn_ref):
    @pl.when(pl.program_id(1) == 0)
    def _():
        x = x_ref[...]
        ms = jnp.mean(x * x, axis=-1, keepdims=True)
        xn = (x * lax.rsqrt(ms + RMS_EPS) * nw_ref[...]).astype(BF16)
        xn_ref[...] = xn
        ba_ref[...] = _nn(xn, wba_ref[...])

    proj_ref[...] = _nn(xn_ref[...], w_ref[...]).astype(BF16)


def _in_proj(x2d, norm_w, w_main, w_ba, tm=512, tn=1024):
    t = x2d.shape[0]
    return pl.pallas_call(
        _in_proj_kernel,
        grid=(t // tm, MAIN_COLS // tn),
        in_specs=[
            pl.BlockSpec((tm, D_MODEL), lambda i, j: (i, 0)),
            pl.BlockSpec((1, D_MODEL), lambda i, j: (0, 0)),
            pl.BlockSpec((D_MODEL, tn), lambda i, j: (0, j)),
            pl.BlockSpec((D_MODEL, LANES), lambda i, j: (0, 0)),
        ],
        out_specs=[
            pl.BlockSpec((tm, tn), lambda i, j: (i, j)),
            pl.BlockSpec((tm, LANES), lambda i, j: (i, 0)),
        ],
        out_shape=[
            jax.ShapeDtypeStruct((t, MAIN_COLS), BF16),
            jax.ShapeDtypeStruct((t, LANES), F32),
        ],
        scratch_shapes=[pltpu.VMEM((tm, D_MODEL), BF16)],
        compiler_params=pltpu.CompilerParams(
            dimension_semantics=("arbitrary", "arbitrary"),
            vmem_limit_bytes=VMEM_LIMIT),
        name="in_proj",
    )(x2d, norm_w, w_main, w_ba)


def _gates_kernel(ba_ref, alog_ref, dtb_ref, out_ref):
    x = ba_ref[0]
    n = x.shape[0]
    lane = lax.broadcasted_iota(jnp.int32, x.shape, 1)
    pos = lax.broadcasted_iota(jnp.int32, x.shape, 0) % CHUNK
    grp = lane // NPAIR
    is_bwd = (lane % NPAIR) >= DN_HEADS

    beta = 1.0 / (1.0 + jnp.exp(-x))
    a = x + dtb_ref[...]
    softplus = jnp.maximum(a, 0.0) + jnp.log(1.0 + jnp.exp(-jnp.abs(a)))
    g = -jnp.exp(alog_ref[...]) * softplus

    pre = g
    suf = g
    s = 1
    while s < CHUNK:
        pre = pre + jnp.where(pos >= s, pltpu.roll(pre, s, axis=0), 0.0)
        suf = suf + jnp.where(pos < CHUNK - s, pltpu.roll(suf, n - s, axis=0), 0.0)
        s *= 2
    tot = pre + suf - g
    gc = jnp.where(is_bwd, suf, pre)
    ghi = gc.astype(BF16).astype(F32)

    out = jnp.where(grp == G_BETA, beta, 0.0)
    out = jnp.where(grp == G_GHI, ghi, out)
    out = jnp.where(grp == G_GLO, gc - ghi, out)
    out = jnp.where(grp == G_EG, jnp.exp(gc), out)
    out = jnp.where(grp == G_ED, jnp.exp(tot - gc), out)
    out = jnp.where(grp == G_EL, jnp.exp(tot), out)
    out_ref[0] = out.astype(BF16)


def _gates(ba3d, alog_lanes, dtb_lanes):
    b, l, _ = ba3d.shape
    return pl.pallas_call(
        _gates_kernel,
        grid=(b,),
        in_specs=[
            pl.BlockSpec((1, l, LANES), lambda i: (i, 0, 0)),
            pl.BlockSpec((1, LANES), lambda i: (0, 0)),
            pl.BlockSpec((1, LANES), lambda i: (0, 0)),
        ],
        out_specs=pl.BlockSpec((1, l, LANES), lambda i: (i, 0, 0)),
        out_shape=jax.ShapeDtypeStruct((b, l, LANES), BF16),
        compiler_params=pltpu.CompilerParams(
            dimension_semantics=("arbitrary",), vmem_limit_bytes=VMEM_LIMIT),
        name="gates",
    )(ba3d, alog_lanes, dtb_lanes)


N_DR = 2 * NA_ROWS - 1
N_DC = 2 * NA_COLS - 1
N_TAB = N_DR - 1


def _na_kernel(rpb_ref, q_ref, k_ref, v_ref, o_ref, tab_ref):
    h = pl.program_id(0)
    rows = q_ref.shape[1] // GRID_W
    band = NA_ROWS * GRID_W

    @pl.when(pl.program_id(1) == 0)
    def _():
        shape = (GRID_W, 2 * GRID_W)
        c = lax.broadcasted_iota(jnp.int32, shape, 0)
        lane = lax.broadcasted_iota(jnp.int32, shape, 1)
        kc = lane % GRID_W
        upper = lane >= GRID_W
        start = jnp.clip(c - NA_COLS // 2, 0, GRID_W - NA_COLS)
        valid = (kc >= start) & (kc < start + NA_COLS)
        delta = kc - c + (NA_COLS - 1)

        def build(d, carry):
            base = (h * N_DR + d) * N_DC
            val = jnp.full(shape, MASK_NEG, F32)
            for dc in range(N_DC):
                s0 = rpb_ref[base + dc]
                s1 = rpb_ref[base + N_DC + dc]
                val = jnp.where(delta == dc, jnp.where(upper, s1, s0), val)
            tab_ref[d] = jnp.where(valid, val, MASK_NEG)
            return carry

        lax.fori_loop(0, N_TAB, build, 0)

    scale = HEAD_DIM ** -0.5

    def row(r, carry):
        rs = jnp.clip(r - NA_ROWS // 2, 0, rows - NA_ROWS)
        d0 = rs - r + (NA_ROWS - 1)
        q = q_ref[0, pl.ds(pl.multiple_of(r * GRID_W, GRID_W), GRID_W), :]
        kb = k_ref[0, pl.ds(pl.multiple_of(rs * GRID_W, GRID_W), band), :]
        vb = v_ref[0, pl.ds(pl.multiple_of(rs * GRID_W, GRID_W), band), :]
        s = _nt(q, kb) * scale
        parts = [s[:, j * LANES:(j + 1) * LANES] + tab_ref[d0 + 2 * j]
                 for j in range(band // LANES)]
        m = parts[0].max(axis=-1, keepdims=True)
        for p_ in parts[1:]:
            m = jnp.maximum(m, p_.max(axis=-1, keepdims=True))
        ps = [jnp.exp(p_ - m) for p_ in parts]
        l = ps[0].sum(axis=-1, keepdims=True)
        for p_ in ps[1:]:
            l = l + p_.sum(axis=-1, keepdims=True)
        p = jnp.concatenate(ps, axis=1).astype(BF16)
        o = _nn(p, vb) / l
        o_ref[0, pl.ds(pl.multiple_of(r * GRID_W, GRID_W), GRID_W), :] = o.astype(BF16)
        return carry

    lax.fori_loop(0, rows, row, 0)


def _na(proj3d, rpb_flat):
    b, l, _ = proj3d.shape
    blk = lambda off: pl.BlockSpec((1, l, HEAD_DIM), lambda h, i, off=off: (i, 0, off + h))
    return pl.pallas_call(
        _na_kernel,
        grid=(NA_HEADS, b),
        in_specs=[
            pl.BlockSpec(memory_space=pltpu.SMEM),
            blk(0), blk(NA_HEADS), blk(2 * NA_HEADS),
        ],
        out_specs=pl.BlockSpec((1, l, HEAD_DIM), lambda h, i: (i, 0, h)),
        out_shape=jax.ShapeDtypeStruct((b, l, NA_WIDTH), BF16),
        scratch_shapes=[pltpu.VMEM((N_TAB, GRID_W, 2 * GRID_W), F32)],
        compiler_params=pltpu.CompilerParams(
            dimension_semantics=("arbitrary", "arbitrary"),
            vmem_limit_bytes=VMEM_LIMIT),
        name="na",
    )(rpb_flat, proj3d, proj3d, proj3d)


SUB = 16


def _conv_silu(x_ref, w_ref):
    x = x_ref[0].astype(F32)
    n = x.shape[0]
    t = lax.broadcasted_iota(jnp.int32, x.shape, 0)
    acc = x * w_ref[CONV_K // 2:CONV_K // 2 + 1, :]
    for i in range(CONV_K):
        off = i - CONV_K // 2
        if off == 0:
            continue
        shifted = pltpu.roll(x, (-off) % n, axis=0)
        ok = (t + off >= 0) & (t + off < n)
        acc = acc + jnp.where(ok, shifted, 0.0) * w_ref[i:i + 1, :]
    return _silu(acc)


def _l2norm(x):
    return x * lax.rsqrt(jnp.sum(x * x, axis=-1, keepdims=True) + L2_EPS)


def _unit_lower_inverse(a):
    n = a.shape[0]
    i = lax.broadcasted_iota(jnp.int32, (n, n), 0)
    j = lax.broadcasted_iota(jnp.int32, (n, n), 1)
    same = (i // SUB) == (j // SUB)
    eye = jnp.where(i == j, 1.0, 0.0)
    d = jnp.where(same, a, 0.0)
    rest = a - d
    dp = d.astype(BF16)
    p = eye - d
    step = 1
    while 2 * step < SUB:
        dpf = _nn(dp, dp)
        dp = dpf.astype(BF16)
        p = p + _nn(p.astype(BF16), dp)
        step *= 2
    m = _nn(p.astype(BF16), rest.astype(BF16))
    mb = m.astype(BF16)
    m2 = _nn(mb, mb).astype(BF16)
    r = (eye - m) + _nn((eye - m).astype(BF16), m2) - eye
    return p + _nn(r.astype(BF16), p.astype(BF16))


def _gdn_kernel(q_ref, k_ref, v_ref, z_ref, cols_ref, wq_ref, wk_ref, wv_ref, nw_ref,
                o_ref, qs, ks, vs, u_s, w_s, qk_s, qg_s, kdt_s, el_s, oacc):
    h = pl.program_id(1)
    l = q_ref.shape[1]
    nchunk = l // CHUNK

    qs[...] = (_l2norm(_conv_silu(q_ref, wq_ref)) * (HEAD_DIM ** -0.5)).astype(BF16)
    ks[...] = _l2norm(_conv_silu(k_ref, wk_ref)).astype(BF16)
    vs[...] = _conv_silu(v_ref, wv_ref).astype(BF16)

    ri = lax.broadcasted_iota(jnp.int32, (CHUNK, CHUNK), 0)
    ci = lax.broadcasted_iota(jnp.int32, (CHUNK, CHUNK), 1)
    sel_r = lax.broadcasted_iota(jnp.int32, (LANES, LANES), 0)
    sel_c = lax.broadcasted_iota(jnp.int32, (CHUNK, LANES), 1)
    eye128 = jnp.where(sel_r == lax.broadcasted_iota(jnp.int32, (LANES, LANES), 1),
                       1.0, 0.0).astype(BF16)

    def onehot_rows(lane_idx):
        return jnp.where(sel_r == lane_idx, 1.0, 0.0).astype(BF16)

    def prep(n, carry):
        r0 = pl.multiple_of(n * CHUNK, CHUNK)
        kc = ks[pl.ds(r0, CHUNK), :]
        qc = qs[pl.ds(r0, CHUNK), :]
        vc = vs[pl.ds(r0, CHUNK), :]
        cols = cols_ref[0, pl.ds(r0, CHUNK), :]
        kk = _nt(kc, kc)
        qk = _nt(qc, kc)
        for d in range(2):
            pair = d * DN_HEADS + h
            bsel = lambda g: _nn(cols, onehot_rows(g * NPAIR + pair))
            beta_b = bsel(G_BETA)
            eg_b = bsel(G_EG)
            ed_b = bsel(G_ED)
            el_b = bsel(G_EL)
            hi, lo = G_GHI * NPAIR + pair, G_GLO * NPAIR + pair
            gsel = jnp.where((sel_r == hi) | (sel_r == lo), 1.0, 0.0).astype(BF16)
            gsel_t = jnp.where((sel_c == hi) | (sel_c == lo), 1.0, 0.0).astype(BF16)
            g_col = _nn(cols, gsel)[:, :CHUNK]
            g_row = _nt(gsel_t, cols)
            diff = g_col - g_row
            if d == 0:
                incl, strict = ri >= ci, ri > ci
            else:
                incl, strict = ri <= ci, ri < ci
            decay = jnp.where(incl, jnp.exp(jnp.where(incl, diff, 0.0)), 0.0)
            beta_c = beta_b[:, :CHUNK]
            a = jnp.where(strict, kk * beta_c * decay, 0.0)
            tmat = _unit_lower_inverse(a).astype(BF16)
            kf = kc.astype(F32)
            vbeta = (vc.astype(F32) * beta_b).astype(BF16)
            kbg = (kf * (beta_b * eg_b)).astype(BF16)
            u_s[d, pl.ds(r0, CHUNK), :] = _nn(tmat, vbeta)
            w_s[d, pl.ds(r0, CHUNK), :] = _nn(tmat, kbg).astype(BF16)
            qk_s[d, pl.ds(r0, CHUNK), :] = jnp.where(incl, qk * decay, 0.0).astype(BF16)
            qg_s[d, pl.ds(r0, CHUNK), :] = (qc.astype(F32) * eg_b).astype(BF16)
            kd = (kf * ed_b).astype(BF16)
            kdt_s[d, n] = _nt(eye128, kd).astype(BF16)
            el_s[d, n] = el_b[:8, :]
        return carry

    lax.fori_loop(0, nchunk, prep, 0)

    def scan(s, carry):
        new = []
        for d in range(2):
            st = carry[d]
            n = s if d == 0 else nchunk - 1 - s
            r0 = pl.multiple_of(n * CHUNK, CHUNK)
            sb = st.astype(BF16)
            v_new = u_s[d, pl.ds(r0, CHUNK), :] - _nn(w_s[d, pl.ds(r0, CHUNK), :], sb)
            vb = v_new.astype(BF16)
            o = _nn(qg_s[d, pl.ds(r0, CHUNK), :], sb) + _nn(qk_s[d, pl.ds(r0, CHUNK), :], vb)
            oacc[d, pl.ds(r0, CHUNK), :] = o
            el = jnp.broadcast_to(el_s[d, n][:1, :], (HEAD_DIM, HEAD_DIM))
            new.append(st * el + _nn(kdt_s[d, n], vb))
        return tuple(new)

    zero = jnp.zeros((HEAD_DIM, HEAD_DIM), F32)
    lax.fori_loop(0, nchunk, scan, (zero, zero))

    o = oacc[0] + oacc[1]
    o = o * lax.rsqrt(jnp.mean(o * o, axis=-1, keepdims=True) + RMS_EPS) * nw_ref[...]
    o_ref[0] = (o * _silu(z_ref[0].astype(F32))).astype(BF16)


def _gdn(proj3d, cols, conv_w, norm_w):
    b, l, _ = proj3d.shape
    nchunk = l // CHUNK
    hb = OFF_DN // HEAD_DIM
    blk = lambda off: pl.BlockSpec((1, l, HEAD_DIM), lambda i, h, off=off: (i, 0, off + h))
    wblk = lambda off: pl.BlockSpec((CONV_K, HEAD_DIM), lambda i, h, off=off: (0, off + h))
    return pl.pallas_call(
        _gdn_kernel,
        grid=(b, DN_HEADS),
        in_specs=[
            blk(hb), blk(hb + DN_HEADS), blk(hb + 2 * DN_HEADS), blk(OFF_Z // HEAD_DIM),
            pl.BlockSpec((1, l, LANES), lambda i, h: (i, 0, 0)),
            wblk(0), wblk(DN_HEADS), wblk(2 * DN_HEADS),
            pl.BlockSpec((1, HEAD_DIM), lambda i, h: (0, 0)),
        ],
        out_specs=pl.BlockSpec((1, l, HEAD_DIM), lambda i, h: (i, 0, h)),
        out_shape=jax.ShapeDtypeStruct((b, l, DN_WIDTH), BF16),
        scratch_shapes=[
            pltpu.VMEM((l, HEAD_DIM), BF16),
            pltpu.VMEM((l, HEAD_DIM), BF16),
            pltpu.VMEM((l, HEAD_DIM), BF16),
            pltpu.VMEM((2, l, HEAD_DIM), F32),
            pltpu.VMEM((2, l, HEAD_DIM), BF16),
            pltpu.VMEM((2, l, CHUNK), BF16),
            pltpu.VMEM((2, l, HEAD_DIM), BF16),
            pltpu.VMEM((2, nchunk, HEAD_DIM, CHUNK), BF16),
            pltpu.VMEM((2, nchunk, 8, HEAD_DIM), F32),
            pltpu.VMEM((2, l, HEAD_DIM), F32),
        ],
        compiler_params=pltpu.CompilerParams(
            dimension_semantics=("arbitrary", "arbitrary"),
            vmem_limit_bytes=VMEM_LIMIT),
        name="gdn",
    )(proj3d, proj3d, proj3d, proj3d, cols, conv_w, conv_w, conv_w, norm_w)


def _out_proj_kernel(x_ref, yna_ref, ydn_ref, wa_ref, wb_ref, nw_ref, h_ref, hn_ref):
    mixed = _nn(yna_ref[...], wa_ref[...]) + _nn(ydn_ref[...], wb_ref[...])
    hres = x_ref[...] + mixed
    h_ref[...] = hres
    ms = jnp.mean(hres * hres, axis=-1, keepdims=True)
    hn_ref[...] = (hres * lax.rsqrt(ms + RMS_EPS) * nw_ref[...]).astype(BF16)


def _out_proj(x2d, yna, ydn, w_a, w_b, norm_w, tm=256):
    t = x2d.shape[0]
    row = lambda w: pl.BlockSpec((tm, w), lambda i: (i, 0))
    full = lambda a: pl.BlockSpec(a.shape, lambda i: (0, 0))
    return pl.pallas_call(
        _out_proj_kernel,
        grid=(t // tm,),
        in_specs=[row(D_MODEL), row(NA_WIDTH), row(DN_WIDTH), full(w_a), full(w_b),
                  full(norm_w)],
        out_specs=[row(D_MODEL), row(D_MODEL)],
        out_shape=[jax.ShapeDtypeStruct((t, D_MODEL), F32),
                   jax.ShapeDtypeStruct((t, D_MODEL), BF16)],
        compiler_params=pltpu.CompilerParams(
            dimension_semantics=("arbitrary",), vmem_limit_bytes=VMEM_LIMIT),
        name="out_proj",
    )(x2d, yna, ydn, w_a, w_b, norm_w)


def _ffn_kernel(hn_ref, h_ref, wg_ref, wu_ref, wd_ref, nw_ref, o_ref, acc_ref):
    f = pl.program_id(1)
    hn = hn_ref[...]
    g = _nn(hn, wg_ref[...])
    u = _nn(hn, wu_ref[...])
    part = _nn((_silu(g) * u).astype(BF16), wd_ref[...])

    @pl.when(f == 0)
    def _():
        acc_ref[...] = h_ref[...] + part

    @pl.when(f > 0)
    def _():
        acc_ref[...] += part

    @pl.when(f == pl.num_programs(1) - 1)
    def _():
        y = acc_ref[...]
        ms = jnp.mean(y * y, axis=-1, keepdims=True)
        o_ref[...] = y * lax.rsqrt(ms + RMS_EPS) * nw_ref[...]


def _ffn(hn, hres, w_gate, w_up, w_down, norm_w, tm=512, tf=512):
    t = hn.shape[0]
    return pl.pallas_call(
        _ffn_kernel,
        grid=(t // tm, D_FF // tf),
        in_specs=[
            pl.BlockSpec((tm, D_MODEL), lambda i, f: (i, 0)),
            pl.BlockSpec((tm, D_MODEL), lambda i, f: (i, 0)),
            pl.BlockSpec((D_MODEL, tf), lambda i, f: (0, f)),
            pl.BlockSpec((D_MODEL, tf), lambda i, f: (0, f)),
            pl.BlockSpec((tf, D_MODEL), lambda i, f: (f, 0)),
            pl.BlockSpec((1, D_MODEL), lambda i, f: (0, 0)),
        ],
        out_specs=pl.BlockSpec((tm, D_MODEL), lambda i, f: (i, 0)),
        out_shape=jax.ShapeDtypeStruct((t, D_MODEL), F32),
        scratch_shapes=[pltpu.VMEM((tm, D_MODEL), F32)],
        compiler_params=pltpu.CompilerParams(
            dimension_semantics=("arbitrary", "arbitrary"),
            vmem_limit_bytes=VMEM_LIMIT),
        name="ffn",
    )(hn, hres, w_gate, w_up, w_down, norm_w)


def _lane_table(per_pair):
    return jnp.tile(per_pair.astype(F32), LANES // NPAIR).reshape(1, LANES)


def _trunk(x, p):
    b, l, _ = x.shape
    x2d = x.reshape(b * l, D_MODEL)
    proj, ba = _in_proj(x2d, p["norm_mix_w"], p["w_main"], p["w_ba"])
    proj3d = proj.reshape(b, l, MAIN_COLS)
    cols = _gates(ba.reshape(b, l, LANES), p["alog_lanes"], p["dtb_lanes"])
    y_na = _na(proj3d, p["rpb_flat"])
    y_dn = _gdn(proj3d, cols, p["conv_w"], p["dn_norm_w"])
    hres, hn = _out_proj(x2d, y_na.reshape(b * l, NA_WIDTH), y_dn.reshape(b * l, DN_WIDTH),
                         p["w_out_na"], p["w_out_dn"], p["norm_ffn_w"])
    y = _ffn(hn, hres, p["w_gate"], p["w_up"], p["w_down"], p["final_norm_w"])
    return y.reshape(b, l, D_MODEL)


def kernel(x_prompt, x_sample, norm_mix_w, w_in, na_rpb, dn_conv_w, dn_A_log, dn_dt_bias,
           dn_norm_w, w_out, norm_ffn_w, w_gate, w_up, w_down, final_norm_w):
    w_in0 = w_in[0]
    w_b = w_in0[:, OFF_B:OFF_A]
    w_a = w_in0[:, OFF_A:OFF_A + NPAIR]
    pad = jnp.zeros((D_MODEL, LANES - 6 * NPAIR), F32)
    p = {
        "norm_mix_w": norm_mix_w[0].reshape(1, D_MODEL),
        "w_main": w_in0[:, :MAIN_COLS].astype(BF16),
        "w_ba": jnp.concatenate([w_b, w_a, w_a, w_a, w_a, w_a, pad], axis=1).astype(BF16),
        "alog_lanes": _lane_table(dn_A_log[0].reshape(NPAIR)),
        "dtb_lanes": _lane_table(dn_dt_bias[0].reshape(NPAIR)),
        "rpb_flat": na_rpb[0].reshape(-1),
        "conv_w": dn_conv_w[0],
        "dn_norm_w": dn_norm_w[0].reshape(1, HEAD_DIM),
        "w_out_na": w_out[0, :NA_WIDTH].astype(BF16),
        "w_out_dn": w_out[0, NA_WIDTH:].astype(BF16),
        "norm_ffn_w": norm_ffn_w[0].reshape(1, D_MODEL),
        "w_gate": w_gate[0].astype(BF16),
        "w_up": w_up[0].astype(BF16),
        "w_down": w_down[0].astype(BF16),
        "final_norm_w": final_norm_w.reshape(1, D_MODEL),
    }
    return (_trunk(x_prompt, p), _trunk(x_sample, p))
```

```python
import functools

import jax
import jax.numpy as jnp
from jax import lax
from jax.experimental import pallas as pl
from jax.experimental.pallas import tpu as pltpu

D_MODEL = 2048
GRID_W = 64
HEAD_DIM = 128
NA_HEADS = 8
DN_HEADS = 8
NA_WIDTH = NA_HEADS * HEAD_DIM
DN_WIDTH = DN_HEADS * HEAD_DIM
NA_ROWS = 8
NA_COLS = 16
CONV_K = 5
CHUNK = 64
D_FF = 5632
RMS_EPS = 1e-6
L2_EPS = 1e-6

OFF_NA = 0
OFF_DN = OFF_NA + 3 * NA_WIDTH
OFF_Z = OFF_DN + 3 * DN_WIDTH
OFF_B = OFF_Z + DN_WIDTH
OFF_A = OFF_B + 2 * DN_HEADS
MAIN_COLS = OFF_B

LANES = 128
NPAIR = 2 * DN_HEADS
G_BETA, G_GHI, G_GLO, G_EG, G_ED, G_EL = range(6)
MASK_NEG = -1e30
VMEM_LIMIT = 56 * 1024 * 1024

F32 = jnp.float32
BF16 = jnp.bfloat16


def _silu(x):
    return x * (1.0 / (1.0 + jnp.exp(-x)))


def _nt(a, b):
    return lax.dot_general(a, b, (((1,), (1,)), ((), ())), preferred_element_type=F32)


def _nn(a, b):
    return jnp.dot(a, b, preferred_element_type=F32)


def _in_proj_kernel(x_ref, nw_ref, w_ref, wba_ref, proj_ref, ba_ref, xn_ref):
    @pl.when(pl.program_id(1) == 0)
    def _():
        x = x_ref[...]
        ms = jnp.mean(x * x, axis=-1, keepdims=True)
        xn = (x * lax.rsqrt(ms + RMS_EPS) * nw_ref[...]).astype(BF16)
        xn_ref[...] = xn
        ba_ref[...] = _nn(xn, wba_ref[...])

    proj_ref[...] = _nn(xn_ref[...], w_ref[...]).astype(BF16)


def _in_proj(x2d, norm_w, w_main, w_ba, tm=512, tn=1024):
    t = x2d.shape[0]
    return pl.pallas_call(
        _in_proj_kernel,
        grid=(t // tm, MAIN_COLS // tn),
        in_specs=[
            pl.BlockSpec((tm, D_MODEL), lambda i, j: (i, 0)),
            pl.BlockSpec((1, D_MODEL), lambda i, j: (0, 0)),
            pl.BlockSpec((D_MODEL, tn), lambda i, j: (0, j)),
            pl.BlockSpec((D_MODEL, LANES), lambda i, j: (0, 0)),
        ],
        out_specs=[
            pl.BlockSpec((tm, tn), lambda i, j: (i, j)),
            pl.BlockSpec((tm, LANES), lambda i, j: (i, 0)),
        ],
        out_shape=[
            jax.ShapeDtypeStruct((t, MAIN_COLS), BF16),
            jax.ShapeDtypeStruct((t, LANES), F32),
        ],
        scratch_shapes=[pltpu.VMEM((tm, D_MODEL), BF16)],
        compiler_params=pltpu.CompilerParams(
            dimension_semantics=("arbitrary", "arbitrary"),
            vmem_limit_bytes=VMEM_LIMIT),
        name="in_proj",
    )(x2d, norm_w, w_main, w_ba)


def _gates_kernel(ba_ref, alog_ref, dtb_ref, out_ref):
    x = ba_ref[0]
    n = x.shape[0]
    lane = lax.broadcasted_iota(jnp.int32, x.shape, 1)
    pos = lax.broadcasted_iota(jnp.int32, x.shape, 0) % CHUNK
    grp = lane // NPAIR
    is_bwd = (lane % NPAIR) >= DN_HEADS

    beta = 1.0 / (1.0 + jnp.exp(-x))
    a = x + dtb_ref[...]
    softplus = jnp.maximum(a, 0.0) + jnp.log(1.0 + jnp.exp(-jnp.abs(a)))
    g = -jnp.exp(alog_ref[...]) * softplus

    pre = g
    suf = g
    s = 1
    while s < CHUNK:
        pre = pre + jnp.where(pos >= s, pltpu.roll(pre, s, axis=0), 0.0)
        suf = suf + jnp.where(pos < CHUNK - s, pltpu.roll(suf, n - s, axis=0), 0.0)
        s *= 2
    tot = pre + suf - g
    gc = jnp.where(is_bwd, suf, pre)
    ghi = gc.astype(BF16).astype(F32)

    out = jnp.where(grp == G_BETA, beta, 0.0)
    out = jnp.where(grp == G_GHI, ghi, out)
    out = jnp.where(grp == G_GLO, gc - ghi, out)
    out = jnp.where(grp == G_EG, jnp.exp(gc), out)
    out = jnp.where(grp == G_ED, jnp.exp(tot - gc), out)
    out = jnp.where(grp == G_EL, jnp.exp(tot), out)
    out_ref[0] = out.astype(BF16)


def _gates(ba3d, alog_lanes, dtb_lanes):
    b, l, _ = ba3d.shape
    return pl.pallas_call(
        _gates_kernel,
        grid=(b,),
        in_specs=[
            pl.BlockSpec((1, l, LANES), lambda i: (i, 0, 0)),
            pl.BlockSpec((1, LANES), lambda i: (0, 0)),
            pl.BlockSpec((1, LANES), lambda i: (0, 0)),
        ],
        out_specs=pl.BlockSpec((1, l, LANES), lambda i: (i, 0, 0)),
        out_shape=jax.ShapeDtypeStruct((b, l, LANES), BF16),
        compiler_params=pltpu.CompilerParams(
            dimension_semantics=("arbitrary",), vmem_limit_bytes=VMEM_LIMIT),
        name="gates",
    )(ba3d, alog_lanes, dtb_lanes)


N_DR = 2 * NA_ROWS - 1
N_DC = 2 * NA_COLS - 1
N_TAB = N_DR - 1


def _na_kernel(rpb_ref, q_ref, k_ref, v_ref, o_ref, tab_ref):
    h = pl.program_id(0)
    rows = q_ref.shape[1] // GRID_W
    band = NA_ROWS * GRID_W

    @pl.when(pl.program_id(1) == 0)
    def _():
        shape = (GRID_W, 2 * GRID_W)
        c = lax.broadcasted_iota(jnp.int32, shape, 0)
        lane = lax.broadcasted_iota(jnp.int32, shape, 1)
        kc = lane % GRID_W
        upper = lane >= GRID_W
        start = jnp.clip(c - NA_COLS // 2, 0, GRID_W - NA_COLS)
        valid = (kc >= start) & (kc < start + NA_COLS)
        delta = kc - c + (NA_COLS - 1)

        def build(d, carry):
            base = (h * N_DR + d) * N_DC
            val = jnp.full(shape, MASK_NEG, F32)
            for dc in range(N_DC):
                s0 = rpb_ref[base + dc]
                s1 = rpb_ref[base + N_DC + dc]
                val = jnp.where(delta == dc, jnp.where(upper, s1, s0), val)
            tab_ref[d] = jnp.where(valid, val, MASK_NEG)
            return carry

        lax.fori_loop(0, N_TAB, build, 0)

    scale = HEAD_DIM ** -0.5

    def row(r, carry):
        rs = jnp.clip(r - NA_ROWS // 2, 0, rows - NA_ROWS)
        d0 = rs - r + (NA_ROWS - 1)
        q = q_ref[0, pl.ds(pl.multiple_of(r * GRID_W, GRID_W), GRID_W), :]
        kb = k_ref[0, pl.ds(pl.multiple_of(rs * GRID_W, GRID_W), band), :]
        vb = v_ref[0, pl.ds(pl.multiple_of(rs * GRID_W, GRID_W), band), :]
        s = _nt(q, kb) * scale
        parts = [s[:, j * LANES:(j + 1) * LANES] + tab_ref[d0 + 2 * j]
                 for j in range(band // LANES)]
        m = parts[0].max(axis=-1, keepdims=True)
        for p_ in parts[1:]:
            m = jnp.maximum(m, p_.max(axis=-1, keepdims=True))
        ps = [jnp.exp(p_ - m) for p_ in parts]
        l = ps[0].sum(axis=-1, keepdims=True)
        for p_ in ps[1:]:
            l = l + p_.sum(axis=-1, keepdims=True)
        p = jnp.concatenate(ps, axis=1).astype(BF16)
        o = _nn(p, vb) / l
        o_ref[0, pl.ds(pl.multiple_of(r * GRID_W, GRID_W), GRID_W), :] = o.astype(BF16)
        return carry

    lax.fori_loop(0, rows, row, 0)


def _na(proj3d, rpb_flat):
    b, l, _ = proj3d.shape
    blk = lambda off: pl.BlockSpec((1, l, HEAD_DIM), lambda h, i, off=off: (i, 0, off + h))
    return pl.pallas_call(
        _na_kernel,
        grid=(NA_HEADS, b),
        in_specs=[
            pl.BlockSpec(memory_space=pltpu.SMEM),
            blk(0), blk(NA_HEADS), blk(2 * NA_HEADS),
        ],
        out_specs=pl.BlockSpec((1, l, HEAD_DIM), lambda h, i: (i, 0, h)),
        out_shape=jax.ShapeDtypeStruct((b, l, NA_WIDTH), BF16),
        scratch_shapes=[pltpu.VMEM((N_TAB, GRID_W, 2 * GRID_W), F32)],
        compiler_params=pltpu.CompilerParams(
            dimension_semantics=("arbitrary", "arbitrary"),
            vmem_limit_bytes=VMEM_LIMIT),
        name="na",
    )(rpb_flat, proj3d, proj3d, proj3d)


SUB = 16


def _conv_silu(x, w):
    n = x.shape[0]
    t = lax.broadcasted_iota(jnp.int32, x.shape, 0)
    acc = x * w[CONV_K // 2:CONV_K // 2 + 1, :]
    for i in range(CONV_K):
        off = i - CONV_K // 2
        if off == 0:
            continue
        shifted = pltpu.roll(x, (-off) % n, axis=0)
        ok = (t + off >= 0) & (t + off < n)
        acc = acc + jnp.where(ok, shifted, 0.0) * w[i:i + 1, :]
    return _silu(acc)


def _l2norm(x):
    return x * lax.rsqrt(jnp.sum(x * x, axis=-1, keepdims=True) + L2_EPS)


def _par(f, *lists):
    return [f(*xs) for xs in zip(*lists)]


def _unit_triangular_inverses(mats):
    n = mats[0].shape[0]
    i = lax.broadcasted_iota(jnp.int32, (n, n), 0)
    j = lax.broadcasted_iota(jnp.int32, (n, n), 1)
    same = (i // SUB) == (j // SUB)
    eye = jnp.where(i == j, 1.0, 0.0)
    bf = lambda x: x.astype(BF16)
    d = _par(lambda a: jnp.where(same, a, 0.0), mats)
    rest = _par(lambda a, dd: bf(a - dd), mats, d)
    dp = _par(bf, d)
    p = _par(lambda dd: eye - dd, d)
    step = 1
    while 2 * step < SUB:
        dp = _par(lambda x: bf(_nn(x, x)), dp)
        p = _par(lambda pp, x: pp + _nn(bf(pp), x), p, dp)
        step *= 2
    m = _par(lambda pp, rr: _nn(bf(pp), rr), p, rest)
    m2 = _par(lambda mm: bf(_nn(bf(mm), bf(mm))), m)
    r = _par(lambda mm, sq: _nn(bf(eye - mm), sq) - mm, m, m2)
    return _par(lambda pp, rr: pp + _nn(bf(rr), bf(pp)), p, r)


GDN_HG = 2
GDN_PREP_G = 4
GDN_CHAINS = 2 * GDN_HG
GDN_W = GDN_HG * HEAD_DIM
MQ_ROWS = HEAD_DIM + CHUNK
BCAST_GROUPS = (G_BETA, G_EG, G_ED, G_EL)
N_BC = len(BCAST_GROUPS)


def _gdn_kernel(q_ref, k_ref, v_ref, z_ref, cols_ref, wq_ref, wk_ref, wv_ref, nw_ref,
                o_ref, qs, ks, vs, esel_s, gselt_s, mq_s, co_s, el_s, st_s, oacc):
    hg = pl.program_id(1)
    l = q_ref.shape[1]
    nchunk = l // CHUNK
    heads = [slice(hh * HEAD_DIM, (hh + 1) * HEAD_DIM) for hh in range(GDN_HG)]

    for cs in heads:
        q = _conv_silu(q_ref[0, :, cs].astype(F32), wq_ref[:, cs])
        qs[:, cs] = (_l2norm(q) * (HEAD_DIM ** -0.5)).astype(BF16)
        k = _conv_silu(k_ref[0, :, cs].astype(F32), wk_ref[:, cs])
        ks[:, cs] = _l2norm(k).astype(BF16)
        vs[:, cs] = _conv_silu(v_ref[0, :, cs].astype(F32), wv_ref[:, cs]).astype(BF16)

    bc_shape = (LANES, N_BC * LANES)
    bc_row = lax.broadcasted_iota(jnp.int32, bc_shape, 0)
    bc_blk = lax.broadcasted_iota(jnp.int32, bc_shape, 1) // LANES
    bc_grp = jnp.zeros(bc_shape, jnp.int32)
    for i, g in enumerate(BCAST_GROUPS):
        bc_grp = jnp.where(bc_blk == i, g, bc_grp)
    sel_r = lax.broadcasted_iota(jnp.int32, (LANES, LANES), 0)
    sel_c = lax.broadcasted_iota(jnp.int32, (CHUNK, LANES), 1)
    for idx in range(GDN_CHAINS):
        pair = (idx % 2) * DN_HEADS + hg * GDN_HG + idx // 2
        hi, lo = G_GHI * NPAIR + pair, G_GLO * NPAIR + pair
        esel_s[idx, :, :N_BC * LANES] = jnp.where(
            bc_row == bc_grp * NPAIR + pair, 1.0, 0.0).astype(BF16)
        esel_s[idx, :, N_BC * LANES:] = jnp.where(
            (sel_r == hi) | (sel_r == lo), 1.0, 0.0).astype(BF16)
        gselt_s[idx] = jnp.where((sel_c == hi) | (sel_c == lo), 1.0, 0.0).astype(BF16)
        st_s[idx] = jnp.zeros((HEAD_DIM, HEAD_DIM), F32)

    ri = lax.broadcasted_iota(jnp.int32, (CHUNK, CHUNK), 0)
    ci = lax.broadcasted_iota(jnp.int32, (CHUNK, CHUNK), 1)
    eye128 = jnp.where(sel_r == lax.broadcasted_iota(jnp.int32, (LANES, LANES), 1),
                       1.0, 0.0).astype(BF16)

    bf = lambda x: x.astype(BF16)
    tri = ((ri >= ci, ri > ci), (ri <= ci, ri < ci))

    def prep(i, carry):
        blocks = [(i * GDN_PREP_G + g, cs) for g in range(GDN_PREP_G) for cs in heads]
        chains = [(b, 2 * (b % GDN_HG) + d, d) for b in range(len(blocks)) for d in range(2)]
        rows = [pl.ds(pl.multiple_of(n * CHUNK, CHUNK), CHUNK) for n, _ in blocks]
        kc = [ks[r, cs] for r, (_, cs) in zip(rows, blocks)]
        qc = [qs[r, cs] for r, (_, cs) in zip(rows, blocks)]
        vf = [vs[r, cs].astype(F32) for r, (_, cs) in zip(rows, blocks)]
        cols = [cols_ref[0, r, :] for r in rows]
        kf = _par(lambda x: x.astype(F32), kc)
        qf = _par(lambda x: x.astype(F32), qc)
        qkk = _par(lambda k_, q_: _nt(jnp.concatenate([k_, q_], axis=0), k_), kc, qc)
        bc = [_nn(cols[b], esel_s[idx, :, :N_BC * LANES]) for b, idx, _ in chains]
        g_col = [_nn(cols[b], esel_s[idx, :, N_BC * LANES:])[:, :CHUNK] for b, idx, _ in chains]
        g_row = [_nt(gselt_s[idx], cols[b]) for b, idx, _ in chains]
        beta_b, eg_b, ed_b, el_b = (
            [x[:, i_ * LANES:(i_ + 1) * LANES] for x in bc] for i_ in range(N_BC))
        decay = [jnp.where(tri[d][0], jnp.exp(jnp.where(tri[d][0], gc_ - gr_, 0.0)), 0.0)
                 for (_, _, d), gc_, gr_ in zip(chains, g_col, g_row)]
        a = [jnp.where(tri[d][1], qkk[b][:CHUNK] * be[:, :CHUNK] * dc, 0.0)
             for (b, _, d), be, dc in zip(chains, beta_b, decay)]
        tmat = _unit_triangular_inverses(a)
        rhs = [jnp.concatenate([bf(kf[b] * (be * eg)), bf(vf[b] * be)], axis=1)
               for (b, _, _), be, eg in zip(chains, beta_b, eg_b)]
        wu = _par(lambda t_, r_: bf(_nn(bf(t_), r_)), tmat, rhs)
        kdt = [bf(_nt(eye128, bf(kf[b] * ed))) for (b, _, _), ed in zip(chains, ed_b)]
        qkm = [bf(jnp.where(tri[d][0], qkk[b][CHUNK:] * dc, 0.0))
               for (b, _, d), dc in zip(chains, decay)]
        res = _par(lambda kt, qm, w_: _nn(jnp.concatenate([kt, qm], axis=0), w_),
                   kdt, qkm, wu)
        for (b, idx, _), rs, eg, el in zip(chains, res, eg_b, el_b):
            n = blocks[b][0]
            mq_s[idx, n, :HEAD_DIM] = bf(-rs[:HEAD_DIM, :HEAD_DIM])
            mq_s[idx, n, HEAD_DIM:] = bf(qf[b] * eg - rs[HEAD_DIM:, :HEAD_DIM])
            co_s[idx, n] = rs[:, HEAD_DIM:]
            el_s[idx, n] = el[:8, :]
        return carry

    lax.fori_loop(0, nchunk // GDN_PREP_G, prep, 0)

    def scan(s, carry):
        for idx in range(GDN_CHAINS):
            n = s if idx % 2 == 0 else nchunk - 1 - s
            st = st_s[idx]
            x = co_s[idx, n] + _nn(mq_s[idx, n], st.astype(BF16))
            oacc[idx, pl.ds(pl.multiple_of(n * CHUNK, CHUNK), CHUNK), :] = x[HEAD_DIM:]
            el = jnp.broadcast_to(el_s[idx, n][:1, :], (HEAD_DIM, HEAD_DIM))
            st_s[idx] = st * el + x[:HEAD_DIM]
        return carry

    lax.fori_loop(0, nchunk, scan, 0)

    for hh, cs in enumerate(heads):
        o = oacc[2 * hh] + oacc[2 * hh + 1]
        o = o * lax.rsqrt(jnp.mean(o * o, axis=-1, keepdims=True) + RMS_EPS) * nw_ref[...]
        o_ref[0, :, cs] = (o * _silu(z_ref[0, :, cs].astype(F32))).astype(BF16)


def _gdn(proj3d, cols, conv_w, norm_w):
    b, l, _ = proj3d.shape
    nchunk = l // CHUNK
    assert DN_HEADS % GDN_HG == 0 and nchunk % GDN_PREP_G == 0
    groups = DN_HEADS // GDN_HG
    blk = lambda off: pl.BlockSpec((1, l, GDN_W), lambda i, h, off=off: (i, 0, off // GDN_W + h))
    wblk = lambda off: pl.BlockSpec((CONV_K, GDN_W), lambda i, h, off=off: (0, off // GDN_W + h))
    return pl.pallas_call(
        _gdn_kernel,
        grid=(b, groups),
        in_specs=[
            blk(OFF_DN), blk(OFF_DN + DN_WIDTH), blk(OFF_DN + 2 * DN_WIDTH), blk(OFF_Z),
            pl.BlockSpec((1, l, LANES), lambda i, h: (i, 0, 0)),
            wblk(0), wblk(DN_WIDTH), wblk(2 * DN_WIDTH),
            pl.BlockSpec((1, HEAD_DIM), lambda i, h: (0, 0)),
        ],
        out_specs=pl.BlockSpec((1, l, GDN_W), lambda i, h: (i, 0, h)),
        out_shape=jax.ShapeDtypeStruct((b, l, DN_WIDTH), BF16),
        scratch_shapes=[
            pltpu.VMEM((l, GDN_W), BF16),
            pltpu.VMEM((l, GDN_W), BF16),
            pltpu.VMEM((l, GDN_W), BF16),
            pltpu.VMEM((GDN_CHAINS, LANES, (N_BC + 1) * LANES), BF16),
            pltpu.VMEM((GDN_CHAINS, CHUNK, LANES), BF16),
            pltpu.VMEM((GDN_CHAINS, nchunk, MQ_ROWS, HEAD_DIM), BF16),
            pltpu.VMEM((GDN_CHAINS, nchunk, MQ_ROWS, HEAD_DIM), F32),
            pltpu.VMEM((GDN_CHAINS, nchunk, 8, HEAD_DIM), F32),
            pltpu.VMEM((GDN_CHAINS, HEAD_DIM, HEAD_DIM), F32),
            pltpu.VMEM((GDN_CHAINS, l, HEAD_DIM), F32),
        ],
        compiler_params=pltpu.CompilerParams(
            dimension_semantics=("arbitrary", "arbitrary"),
            vmem_limit_bytes=VMEM_LIMIT),
        name="gdn",
    )(proj3d, proj3d, proj3d, proj3d, cols, conv_w, conv_w, conv_w, norm_w)


def _out_proj_kernel(x_ref, yna_ref, ydn_ref, wa_ref, wb_ref, nw_ref, h_ref, hn_ref):
    mixed = _nn(yna_ref[...], wa_ref[...]) + _nn(ydn_ref[...], wb_ref[...])
    hres = x_ref[...] + mixed
    h_ref[...] = hres
    ms = jnp.mean(hres * hres, axis=-1, keepdims=True)
    hn_ref[...] = (hres * lax.rsqrt(ms + RMS_EPS) * nw_ref[...]).astype(BF16)


def _out_proj(x2d, yna, ydn, w_a, w_b, norm_w, tm=256):
    t = x2d.shape[0]
    row = lambda w: pl.BlockSpec((tm, w), lambda i: (i, 0))
    full = lambda a: pl.BlockSpec(a.shape, lambda i: (0, 0))
    return pl.pallas_call(
        _out_proj_kernel,
        grid=(t // tm,),
        in_specs=[row(D_MODEL), row(NA_WIDTH), row(DN_WIDTH), full(w_a), full(w_b),
                  full(norm_w)],
        out_specs=[row(D_MODEL), row(D_MODEL)],
        out_shape=[jax.ShapeDtypeStruct((t, D_MODEL), F32),
                   jax.ShapeDtypeStruct((t, D_MODEL), BF16)],
        compiler_params=pltpu.CompilerParams(
            dimension_semantics=("arbitrary",), vmem_limit_bytes=VMEM_LIMIT),
        name="out_proj",
    )(x2d, yna, ydn, w_a, w_b, norm_w)


def _ffn_kernel(hn_ref, h_ref, wg_ref, wu_ref, wd_ref, nw_ref, o_ref, acc_ref):
    f = pl.program_id(1)
    hn = hn_ref[...]
    g = _nn(hn, wg_ref[...])
    u = _nn(hn, wu_ref[...])
    part = _nn((_silu(g) * u).astype(BF16), wd_ref[...])

    @pl.when(f == 0)
    def _():
        acc_ref[...] = h_ref[...] + part

    @pl.when(f > 0)
    def _():
        acc_ref[...] += part

    @pl.when(f == pl.num_programs(1) - 1)
    def _():
        y = acc_ref[...]
        ms = jnp.mean(y * y, axis=-1, keepdims=True)
        o_ref[...] = y * lax.rsqrt(ms + RMS_EPS) * nw_ref[...]


def _ffn(hn, hres, w_gate, w_up, w_down, norm_w, tm=512, tf=512):
    t = hn.shape[0]
    return pl.pallas_call(
        _ffn_kernel,
        grid=(t // tm, D_FF // tf),
        in_specs=[
            pl.BlockSpec((tm, D_MODEL), lambda i, f: (i, 0)),
            pl.BlockSpec((tm, D_MODEL), lambda i, f: (i, 0)),
            pl.BlockSpec((D_MODEL, tf), lambda i, f: (0, f)),
            pl.BlockSpec((D_MODEL, tf), lambda i, f: (0, f)),
            pl.BlockSpec((tf, D_MODEL), lambda i, f: (f, 0)),
            pl.BlockSpec((1, D_MODEL), lambda i, f: (0, 0)),
        ],
        out_specs=pl.BlockSpec((tm, D_MODEL), lambda i, f: (i, 0)),
        out_shape=jax.ShapeDtypeStruct((t, D_MODEL), F32),
        scratch_shapes=[pltpu.VMEM((tm, D_MODEL), F32)],
        compiler_params=pltpu.CompilerParams(
            dimension_semantics=("arbitrary", "arbitrary"),
            vmem_limit_bytes=VMEM_LIMIT),
        name="ffn",
    )(hn, hres, w_gate, w_up, w_down, norm_w)


def _lane_table(per_pair):
    return jnp.tile(per_pair.astype(F32), LANES // NPAIR).reshape(1, LANES)


def _trunk(x, p):
    b, l, _ = x.shape
    x2d = x.reshape(b * l, D_MODEL)
    proj, ba = _in_proj(x2d, p["norm_mix_w"], p["w_main"], p["w_ba"])
    proj3d = proj.reshape(b, l, MAIN_COLS)
    cols = _gates(ba.reshape(b, l, LANES), p["alog_lanes"], p["dtb_lanes"])
    y_na = _na(proj3d, p["rpb_flat"])
    y_dn = _gdn(proj3d, cols, p["conv_w"], p["dn_norm_w"])
    hres, hn = _out_proj(x2d, y_na.reshape(b * l, NA_WIDTH), y_dn.reshape(b * l, DN_WIDTH),
                         p["w_out_na"], p["w_out_dn"], p["norm_ffn_w"])
    y = _ffn(hn, hres, p["w_gate"], p["w_up"], p["w_down"], p["final_norm_w"])
    return y.reshape(b, l, D_MODEL)


def kernel(x_prompt, x_sample, norm_mix_w, w_in, na_rpb, dn_conv_w, dn_A_log, dn_dt_bias,
           dn_norm_w, w_out, norm_ffn_w, w_gate, w_up, w_down, final_norm_w):
    w_in0 = w_in[0]
    w_b = w_in0[:, OFF_B:OFF_A]
    w_a = w_in0[:, OFF_A:OFF_A + NPAIR]
    pad = jnp.zeros((D_MODEL, LANES - 6 * NPAIR), F32)
    p = {
        "norm_mix_w": norm_mix_w[0].reshape(1, D_MODEL),
        "w_main": w_in0[:, :MAIN_COLS].astype(BF16),
        "w_ba": jnp.concatenate([w_b, w_a, w_a, w_a, w_a, w_a, pad], axis=1).astype(BF16),
        "alog_lanes": _lane_table(dn_A_log[0].reshape(NPAIR)),
        "dtb_lanes": _lane_table(dn_dt_bias[0].reshape(NPAIR)),
        "rpb_flat": na_rpb[0].reshape(-1),
        "conv_w": dn_conv_w[0],
        "dn_norm_w": dn_norm_w[0].reshape(1, HEAD_DIM),
        "w_out_na": w_out[0, :NA_WIDTH].astype(BF16),
        "w_out_dn": w_out[0, NA_WIDTH:].astype(BF16),
        "norm_ffn_w": norm_ffn_w[0].reshape(1, D_MODEL),
        "w_gate": w_gate[0].astype(BF16),
        "w_up": w_up[0].astype(BF16),
        "w_down": w_down[0].astype(BF16),
        "final_norm_w": final_norm_w.reshape(1, D_MODEL),
    }
    return (_trunk(x_prompt, p), _trunk(x_sample, p))
```

```python
import functools

import jax
import jax.numpy as jnp
from jax import lax
from jax.experimental import pallas as pl
from jax.experimental.pallas import tpu as pltpu

D_MODEL = 2048
GRID_W = 64
HEAD_DIM = 128
NA_HEADS = 8
DN_HEADS = 8
NA_WIDTH = NA_HEADS * HEAD_DIM
DN_WIDTH = DN_HEADS * HEAD_DIM
NA_ROWS = 8
NA_COLS = 16
CONV_K = 5
CHUNK = 64
D_FF = 5632
RMS_EPS = 1e-6
L2_EPS = 1e-6

OFF_NA = 0
OFF_DN = OFF_NA + 3 * NA_WIDTH
OFF_Z = OFF_DN + 3 * DN_WIDTH
OFF_B = OFF_Z + DN_WIDTH
OFF_A = OFF_B + 2 * DN_HEADS
MAIN_COLS = OFF_B

LANES = 128
NPAIR = 2 * DN_HEADS
G_BETA, G_GHI, G_GLO, G_EG, G_ED, G_EL = range(6)
MASK_NEG = -1e30
VMEM_LIMIT = 56 * 1024 * 1024

F32 = jnp.float32
BF16 = jnp.bfloat16


def _silu(x):
    return x * (1.0 / (1.0 + jnp.exp(-x)))


def _nt(a, b):
    return lax.dot_general(a, b, (((1,), (1,)), ((), ())), preferred_element_type=F32)


def _nn(a, b):
    return jnp.dot(a, b, preferred_element_type=F32)


def _in_proj_kernel(x_ref, nw_ref, w_ref, wba_ref, proj_ref, ba_ref, xn_ref):
    @pl.when(pl.program_id(1) == 0)
    def _():
        x = x_ref[...]
        ms = jnp.mean(x * x, axis=-1, keepdims=True)
        xn = (x * lax.rsqrt(ms + RMS_EPS) * nw_ref[...]).astype(BF16)
        xn_ref[...] = xn
        ba_ref[...] = _nn(xn, wba_ref[...])

    proj_ref[...] = _nn(xn_ref[...], w_ref[...]).astype(BF16)


def _in_proj(x2d, norm_w, w_main, w_ba, tm=1024, tn=1024):
    t = x2d.shape[0]
    return pl.pallas_call(
        _in_proj_kernel,
        grid=(t // tm, MAIN_COLS // tn),
        in_specs=[
            pl.BlockSpec((tm, D_MODEL), lambda i, j: (i, 0)),
            pl.BlockSpec((1, D_MODEL), lambda i, j: (0, 0)),
            pl.BlockSpec((D_MODEL, tn), lambda i, j: (0, j)),
            pl.BlockSpec((D_MODEL, LANES), lambda i, j: (0, 0)),
        ],
        out_specs=[
            pl.BlockSpec((tm, tn), lambda i, j: (i, j)),
            pl.BlockSpec((tm, LANES), lambda i, j: (i, 0)),
        ],
        out_shape=[
            jax.ShapeDtypeStruct((t, MAIN_COLS), BF16),
            jax.ShapeDtypeStruct((t, LANES), F32),
        ],
        scratch_shapes=[pltpu.VMEM((tm, D_MODEL), BF16)],
        compiler_params=pltpu.CompilerParams(
            dimension_semantics=("arbitrary", "arbitrary"),
            vmem_limit_bytes=VMEM_LIMIT),
        name="in_proj",
    )(x2d, norm_w, w_main, w_ba)


def _gates_kernel(ba_ref, alog_ref, dtb_ref, out_ref):
    x = ba_ref[0]
    n = x.shape[0]
    lane = lax.broadcasted_iota(jnp.int32, x.shape, 1)
    pos = lax.broadcasted_iota(jnp.int32, x.shape, 0) % CHUNK
    grp = lane // NPAIR
    is_bwd = (lane % NPAIR) >= DN_HEADS

    beta = 1.0 / (1.0 + jnp.exp(-x))
    a = x + dtb_ref[...]
    softplus = jnp.maximum(a, 0.0) + jnp.log(1.0 + jnp.exp(-jnp.abs(a)))
    g = -jnp.exp(alog_ref[...]) * softplus

    pre = g
    suf = g
    s = 1
    while s < CHUNK:
        pre = pre + jnp.where(pos >= s, pltpu.roll(pre, s, axis=0), 0.0)
        suf = suf + jnp.where(pos < CHUNK - s, pltpu.roll(suf, n - s, axis=0), 0.0)
        s *= 2
    tot = pre + suf - g
    gc = jnp.where(is_bwd, suf, pre)
    ghi = gc.astype(BF16).astype(F32)

    out = jnp.where(grp == G_BETA, beta, 0.0)
    out = jnp.where(grp == G_GHI, ghi, out)
    out = jnp.where(grp == G_GLO, gc - ghi, out)
    out = jnp.where(grp == G_EG, jnp.exp(gc), out)
    out = jnp.where(grp == G_ED, jnp.exp(tot - gc), out)
    out = jnp.where(grp == G_EL, jnp.exp(tot), out)
    out_ref[0] = out.astype(BF16)


def _gates(ba3d, alog_lanes, dtb_lanes):
    b, l, _ = ba3d.shape
    return pl.pallas_call(
        _gates_kernel,
        grid=(b,),
        in_specs=[
            pl.BlockSpec((1, l, LANES), lambda i: (i, 0, 0)),
            pl.BlockSpec((1, LANES), lambda i: (0, 0)),
            pl.BlockSpec((1, LANES), lambda i: (0, 0)),
        ],
        out_specs=pl.BlockSpec((1, l, LANES), lambda i: (i, 0, 0)),
        out_shape=jax.ShapeDtypeStruct((b, l, LANES), BF16),
        compiler_params=pltpu.CompilerParams(
            dimension_semantics=("arbitrary",), vmem_limit_bytes=VMEM_LIMIT),
        name="gates",
    )(ba3d, alog_lanes, dtb_lanes)


N_DR = 2 * NA_ROWS - 1
N_DC = 2 * NA_COLS - 1
N_TAB = N_DR - 1
NA_ROW_G = 8


def _na_kernel(rpb_ref, q_ref, k_ref, v_ref, o_ref, tab_ref):
    h = pl.program_id(0)
    rows = q_ref.shape[1] // GRID_W
    band = NA_ROWS * GRID_W

    @pl.when(pl.program_id(1) == 0)
    def _():
        shape = (GRID_W, 2 * GRID_W)
        c = lax.broadcasted_iota(jnp.int32, shape, 0)
        lane = lax.broadcasted_iota(jnp.int32, shape, 1)
        kc = lane % GRID_W
        upper = lane >= GRID_W
        start = jnp.clip(c - NA_COLS // 2, 0, GRID_W - NA_COLS)
        valid = (kc >= start) & (kc < start + NA_COLS)
        delta = kc - c + (NA_COLS - 1)

        def build(d, carry):
            base = (h * N_DR + d) * N_DC
            val = jnp.full(shape, MASK_NEG, F32)
            for dc in range(N_DC):
                s0 = rpb_ref[base + dc]
                s1 = rpb_ref[base + N_DC + dc]
                val = jnp.where(delta == dc, jnp.where(upper, s1, s0), val)
            tab_ref[d] = jnp.where(valid, val, MASK_NEG)
            return carry

        lax.fori_loop(0, N_TAB, build, 0)

    scale = HEAD_DIM ** -0.5

    nparts = band // LANES

    def row_group(i, carry):
        rr = [i * NA_ROW_G + g for g in range(NA_ROW_G)]
        rs = [jnp.clip(r - NA_ROWS // 2, 0, rows - NA_ROWS) for r in rr]
        d0 = [s_ - r + (NA_ROWS - 1) for s_, r in zip(rs, rr)]
        qrow = [pl.ds(pl.multiple_of(r * GRID_W, GRID_W), GRID_W) for r in rr]
        brow = [pl.ds(pl.multiple_of(s_ * GRID_W, GRID_W), band) for s_ in rs]
        s = [_nt(q_ref[0, qr, :], k_ref[0, br, :]) for qr, br in zip(qrow, brow)]
        parts = [[x[:, j * LANES:(j + 1) * LANES] * scale + tab_ref[d + 2 * j]
                  for j in range(nparts)] for x, d in zip(s, d0)]
        m = [functools.reduce(jnp.maximum, ps_).max(axis=-1, keepdims=True) for ps_ in parts]
        e = [[jnp.exp(p_ - m_) for p_ in ps_] for ps_, m_ in zip(parts, m)]
        l = [functools.reduce(jnp.add, es_).sum(axis=-1, keepdims=True) for es_ in e]
        o = [_nn(jnp.concatenate(es_, axis=1).astype(BF16), v_ref[0, br, :])
             for es_, br in zip(e, brow)]
        for qr, o_, l_ in zip(qrow, o, l):
            o_ref[0, qr, :] = (o_ / l_).astype(BF16)
        return carry

    lax.fori_loop(0, rows // NA_ROW_G, row_group, 0)


def _na(proj3d, rpb_flat):
    b, l, _ = proj3d.shape
    blk = lambda off: pl.BlockSpec((1, l, HEAD_DIM), lambda h, i, off=off: (i, 0, off + h))
    return pl.pallas_call(
        _na_kernel,
        grid=(NA_HEADS, b),
        in_specs=[
            pl.BlockSpec(memory_space=pltpu.SMEM),
            blk(0), blk(NA_HEADS), blk(2 * NA_HEADS),
        ],
        out_specs=pl.BlockSpec((1, l, HEAD_DIM), lambda h, i: (i, 0, h)),
        out_shape=jax.ShapeDtypeStruct((b, l, NA_WIDTH), BF16),
        scratch_shapes=[pltpu.VMEM((N_TAB, GRID_W, 2 * GRID_W), F32)],
        compiler_params=pltpu.CompilerParams(
            dimension_semantics=("arbitrary", "arbitrary"),
            vmem_limit_bytes=VMEM_LIMIT),
        name="na",
    )(rpb_flat, proj3d, proj3d, proj3d)


SUB = 16


CONV_PAD = 8


CONV_TILE = 256


def _conv_silu(x_ref, cs, w, pad_ref, finish):
    n = x_ref.shape[1]
    half = CONV_K // 2

    def fill(t, carry):
        r0 = pl.multiple_of(t * CONV_TILE, CONV_TILE)
        pad_ref[pl.ds(CONV_PAD + r0, CONV_TILE), :] = (
            x_ref[0, pl.ds(r0, CONV_TILE), cs].astype(F32))
        return carry

    def tile(t, carry):
        r0 = pl.multiple_of(t * CONV_TILE, CONV_TILE)
        acc = None
        for i in range(CONV_K):
            term = pad_ref[pl.ds(r0 + (CONV_PAD + i - half), CONV_TILE), :] * w[i:i + 1, :]
            acc = term if acc is None else acc + term
        finish(r0, _silu(acc))
        return carry

    lax.fori_loop(0, n // CONV_TILE, fill, 0)
    lax.fori_loop(0, n // CONV_TILE, tile, 0, unroll=2)


def _l2norm(x):
    return x * lax.rsqrt(jnp.sum(x * x, axis=-1, keepdims=True) + L2_EPS)


def _par(f, *lists):
    return [f(*xs) for xs in zip(*lists)]


def _unit_triangular_inverses(mats):
    n = mats[0].shape[0]
    i = lax.broadcasted_iota(jnp.int32, (n, n), 0)
    j = lax.broadcasted_iota(jnp.int32, (n, n), 1)
    same = (i // SUB) == (j // SUB)
    eye = jnp.where(i == j, 1.0, 0.0)
    bf = lambda x: x.astype(BF16)
    d = _par(lambda a: jnp.where(same, a, 0.0), mats)
    rest = _par(lambda a, dd: bf(a - dd), mats, d)
    dp = _par(bf, d)
    p = _par(lambda dd: eye - dd, d)
    step = 1
    while 2 * step < SUB:
        dp = _par(lambda x: bf(_nn(x, x)), dp)
        p = _par(lambda pp, x: pp + _nn(bf(pp), x), p, dp)
        step *= 2
    m = _par(lambda pp, rr: _nn(bf(pp), rr), p, rest)
    m2 = _par(lambda mm: bf(_nn(bf(mm), bf(mm))), m)
    r = _par(lambda mm, sq: _nn(bf(eye - mm), sq) - mm, m, m2)
    return _par(lambda pp, rr: pp + _nn(bf(rr), bf(pp)), p, r)


GDN_HG = 2
GDN_PREP_G = 8
GDN_CHAINS = 2 * GDN_HG
GDN_W = GDN_HG * HEAD_DIM
MQ_ROWS = HEAD_DIM + CHUNK
BCAST_GROUPS = (G_BETA, G_EG, G_ED, G_EL)
N_BC = len(BCAST_GROUPS)


def _gdn_kernel(q_ref, k_ref, v_ref, z_ref, cols_ref, wq_ref, wk_ref, wv_ref, nw_ref,
                o_ref, pad_s, qs, ks, vs, esel_s, gselt_s, mq_s, co_s, el_s, st_s, oacc):
    hg = pl.program_id(1)
    l = q_ref.shape[1]
    nchunk = l // CHUNK
    heads = [slice(hh * HEAD_DIM, (hh + 1) * HEAD_DIM) for hh in range(GDN_HG)]

    zero_pad = jnp.zeros((CONV_PAD, HEAD_DIM), F32)
    pad_s[:CONV_PAD, :] = zero_pad
    pad_s[CONV_PAD + l:, :] = zero_pad
    for cs in heads:
        def put_q(r0, y, cs=cs):
            qs[pl.ds(r0, CONV_TILE), cs] = (_l2norm(y) * (HEAD_DIM ** -0.5)).astype(BF16)

        def put_k(r0, y, cs=cs):
            ks[pl.ds(r0, CONV_TILE), cs] = _l2norm(y).astype(BF16)

        def put_v(r0, y, cs=cs):
            vs[pl.ds(r0, CONV_TILE), cs] = y.astype(BF16)

        _conv_silu(q_ref, cs, wq_ref[:, cs], pad_s, put_q)
        _conv_silu(k_ref, cs, wk_ref[:, cs], pad_s, put_k)
        _conv_silu(v_ref, cs, wv_ref[:, cs], pad_s, put_v)

    bc_shape = (LANES, N_BC * LANES)
    bc_row = lax.broadcasted_iota(jnp.int32, bc_shape, 0)
    bc_blk = lax.broadcasted_iota(jnp.int32, bc_shape, 1) // LANES
    bc_grp = jnp.zeros(bc_shape, jnp.int32)
    for i, g in enumerate(BCAST_GROUPS):
        bc_grp = jnp.where(bc_blk == i, g, bc_grp)
    sel_r = lax.broadcasted_iota(jnp.int32, (LANES, LANES), 0)
    sel_c = lax.broadcasted_iota(jnp.int32, (CHUNK, LANES), 1)
    for idx in range(GDN_CHAINS):
        pair = (idx % 2) * DN_HEADS + hg * GDN_HG + idx // 2
        hi, lo = G_GHI * NPAIR + pair, G_GLO * NPAIR + pair
        esel_s[idx, :, :N_BC * LANES] = jnp.where(
            bc_row == bc_grp * NPAIR + pair, 1.0, 0.0).astype(BF16)
        esel_s[idx, :, N_BC * LANES:] = jnp.where(
            (sel_r == hi) | (sel_r == lo), 1.0, 0.0).astype(BF16)
        gselt_s[idx] = jnp.where((sel_c == hi) | (sel_c == lo), 1.0, 0.0).astype(BF16)
        st_s[idx] = jnp.zeros((HEAD_DIM, HEAD_DIM), F32)

    ri = lax.broadcasted_iota(jnp.int32, (CHUNK, CHUNK), 0)
    ci = lax.broadcasted_iota(jnp.int32, (CHUNK, CHUNK), 1)
    eye128 = jnp.where(sel_r == lax.broadcasted_iota(jnp.int32, (LANES, LANES), 1),
                       1.0, 0.0).astype(BF16)

    bf = lambda x: x.astype(BF16)
    tri = ((ri >= ci, ri > ci), (ri <= ci, ri < ci))

    def prep(i, carry):
        blocks = [(i * GDN_PREP_G + g, cs) for g in range(GDN_PREP_G) for cs in heads]
        chains = [(b, 2 * (b % GDN_HG) + d, d) for b in range(len(blocks)) for d in range(2)]
        rows = [pl.ds(pl.multiple_of(n * CHUNK, CHUNK), CHUNK) for n, _ in blocks]
        kc = [ks[r, cs] for r, (_, cs) in zip(rows, blocks)]
        qc = [qs[r, cs] for r, (_, cs) in zip(rows, blocks)]
        vf = [vs[r, cs].astype(F32) for r, (_, cs) in zip(rows, blocks)]
        cols = [cols_ref[0, r, :] for r in rows]
        kf = _par(lambda x: x.astype(F32), kc)
        qf = _par(lambda x: x.astype(F32), qc)
        qkk = _par(lambda k_, q_: _nt(jnp.concatenate([k_, q_], axis=0), k_), kc, qc)
        bc = [_nn(cols[b], esel_s[idx, :, :N_BC * LANES]) for b, idx, _ in chains]
        g_bc = [_nn(cols[b], esel_s[idx, :, N_BC * LANES:]) for b, idx, _ in chains]
        g_col = [x[:, :CHUNK] for x in g_bc]
        g_row = [x.T[:CHUNK, :] for x in g_bc]
        beta_b, eg_b, ed_b, el_b = (
            [x[:, i_ * LANES:(i_ + 1) * LANES] for x in bc] for i_ in range(N_BC))
        decay = [jnp.where(tri[d][0], jnp.exp(jnp.where(tri[d][0], gc_ - gr_, 0.0)), 0.0)
                 for (_, _, d), gc_, gr_ in zip(chains, g_col, g_row)]
        a = [jnp.where(tri[d][1], qkk[b][:CHUNK] * be[:, :CHUNK] * dc, 0.0)
             for (b, _, d), be, dc in zip(chains, beta_b, decay)]
        tmat = _unit_triangular_inverses(a)
        rhs = [jnp.concatenate([bf(kf[b] * (be * eg)), bf(vf[b] * be)], axis=1)
               for (b, _, _), be, eg in zip(chains, beta_b, eg_b)]
        wu = _par(lambda t_, r_: bf(_nn(bf(t_), r_)), tmat, rhs)
        kdt = [bf((kf[b] * ed).T) for (b, _, _), ed in zip(chains, ed_b)]
        qkm = [bf(jnp.where(tri[d][0], qkk[b][CHUNK:] * dc, 0.0))
               for (b, _, d), dc in zip(chains, decay)]
        res = _par(lambda kt, qm, w_: _nn(jnp.concatenate([kt, qm], axis=0), w_),
                   kdt, qkm, wu)
        for (b, idx, _), rs, eg, el in zip(chains, res, eg_b, el_b):
            n = blocks[b][0]
            mq_s[idx, n, :HEAD_DIM] = bf(-rs[:HEAD_DIM, :HEAD_DIM])
            mq_s[idx, n, HEAD_DIM:] = bf(qf[b] * eg - rs[HEAD_DIM:, :HEAD_DIM])
            co_s[idx, n] = rs[:, HEAD_DIM:]
            el_s[idx, n] = el[:8, :]
        return carry

    lax.fori_loop(0, nchunk // GDN_PREP_G, prep, 0)

    def scan(s, carry):
        for idx in range(GDN_CHAINS):
            n = s if idx % 2 == 0 else nchunk - 1 - s
            st = st_s[idx]
            x = co_s[idx, n] + _nn(mq_s[idx, n], st.astype(BF16))
            oacc[idx, pl.ds(pl.multiple_of(n * CHUNK, CHUNK), CHUNK), :] = x[HEAD_DIM:]
            el = jnp.broadcast_to(el_s[idx, n][:1, :], (HEAD_DIM, HEAD_DIM))
            st_s[idx] = st * el + x[:HEAD_DIM]
        return carry

    lax.fori_loop(0, nchunk, scan, 0)

    for hh, cs in enumerate(heads):
        o = oacc[2 * hh] + oacc[2 * hh + 1]
        o = o * lax.rsqrt(jnp.mean(o * o, axis=-1, keepdims=True) + RMS_EPS) * nw_ref[...]
        o_ref[0, :, cs] = (o * _silu(z_ref[0, :, cs].astype(F32))).astype(BF16)


def _gdn(proj3d, cols, conv_w, norm_w):
    b, l, _ = proj3d.shape
    nchunk = l // CHUNK
    assert DN_HEADS % GDN_HG == 0 and nchunk % GDN_PREP_G == 0
    groups = DN_HEADS // GDN_HG
    blk = lambda off: pl.BlockSpec((1, l, GDN_W), lambda i, h, off=off: (i, 0, off // GDN_W + h))
    wblk = lambda off: pl.BlockSpec((CONV_K, GDN_W), lambda i, h, off=off: (0, off // GDN_W + h))
    return pl.pallas_call(
        _gdn_kernel,
        grid=(b, groups),
        in_specs=[
            blk(OFF_DN), blk(OFF_DN + DN_WIDTH), blk(OFF_DN + 2 * DN_WIDTH), blk(OFF_Z),
            pl.BlockSpec((1, l, LANES), lambda i, h: (i, 0, 0)),
            wblk(0), wblk(DN_WIDTH), wblk(2 * DN_WIDTH),
            pl.BlockSpec((1, HEAD_DIM), lambda i, h: (0, 0)),
        ],
        out_specs=pl.BlockSpec((1, l, GDN_W), lambda i, h: (i, 0, h)),
        out_shape=jax.ShapeDtypeStruct((b, l, DN_WIDTH), BF16),
        scratch_shapes=[
            pltpu.VMEM((l + 2 * CONV_PAD, HEAD_DIM), F32),
            pltpu.VMEM((l, GDN_W), BF16),
            pltpu.VMEM((l, GDN_W), BF16),
            pltpu.VMEM((l, GDN_W), BF16),
            pltpu.VMEM((GDN_CHAINS, LANES, (N_BC + 1) * LANES), BF16),
            pltpu.VMEM((GDN_CHAINS, CHUNK, LANES), BF16),
            pltpu.VMEM((GDN_CHAINS, nchunk, MQ_ROWS, HEAD_DIM), BF16),
            pltpu.VMEM((GDN_CHAINS, nchunk, MQ_ROWS, HEAD_DIM), F32),
            pltpu.VMEM((GDN_CHAINS, nchunk, 8, HEAD_DIM), F32),
            pltpu.VMEM((GDN_CHAINS, HEAD_DIM, HEAD_DIM), F32),
            pltpu.VMEM((GDN_CHAINS, l, HEAD_DIM), F32),
        ],
        compiler_params=pltpu.CompilerParams(
            dimension_semantics=("arbitrary", "arbitrary"),
            vmem_limit_bytes=VMEM_LIMIT),
        name="gdn",
    )(proj3d, proj3d, proj3d, proj3d, cols, conv_w, conv_w, conv_w, norm_w)


def _out_proj_kernel(x_ref, yna_ref, ydn_ref, wa_ref, wb_ref, nw_ref, h_ref, hn_ref):
    mixed = _nn(yna_ref[...], wa_ref[...]) + _nn(ydn_ref[...], wb_ref[...])
    hres = x_ref[...] + mixed
    h_ref[...] = hres
    ms = jnp.mean(hres * hres, axis=-1, keepdims=True)
    hn_ref[...] = (hres * lax.rsqrt(ms + RMS_EPS) * nw_ref[...]).astype(BF16)


def _out_proj(x2d, yna, ydn, w_a, w_b, norm_w, tm=256):
    t = x2d.shape[0]
    row = lambda w: pl.BlockSpec((tm, w), lambda i: (i, 0))
    full = lambda a: pl.BlockSpec(a.shape, lambda i: (0, 0))
    return pl.pallas_call(
        _out_proj_kernel,
        grid=(t // tm,),
        in_specs=[row(D_MODEL), row(NA_WIDTH), row(DN_WIDTH), full(w_a), full(w_b),
                  full(norm_w)],
        out_specs=[row(D_MODEL), row(D_MODEL)],
        out_shape=[jax.ShapeDtypeStruct((t, D_MODEL), F32),
                   jax.ShapeDtypeStruct((t, D_MODEL), BF16)],
        compiler_params=pltpu.CompilerParams(
            dimension_semantics=("arbitrary",), vmem_limit_bytes=VMEM_LIMIT),
        name="out_proj",
    )(x2d, yna, ydn, w_a, w_b, norm_w)


def _ffn_kernel(hn_ref, h_ref, wg_ref, wu_ref, wd_ref, nw_ref, o_ref, acc_ref):
    f = pl.program_id(1)

    @pl.when(f == 0)
    def _():
        acc_ref[...] = h_ref[...]

    hn = hn_ref[...]
    g = _nn(hn, wg_ref[...])
    u = _nn(hn, wu_ref[...])
    acc_ref[...] += _nn((_silu(g) * u).astype(BF16), wd_ref[...])

    @pl.when(f == pl.num_programs(1) - 1)
    def _():
        y = acc_ref[...]
        ms = jnp.mean(y * y, axis=-1, keepdims=True)
        o_ref[...] = y * lax.rsqrt(ms + RMS_EPS) * nw_ref[...]


def _ffn(hn, hres, w_gate, w_up, w_down, norm_w, tm=512, tf=512):
    t = hn.shape[0]
    return pl.pallas_call(
        _ffn_kernel,
        grid=(t // tm, D_FF // tf),
        in_specs=[
            pl.BlockSpec((tm, D_MODEL), lambda i, f: (i, 0)),
            pl.BlockSpec((tm, D_MODEL), lambda i, f: (i, 0)),
            pl.BlockSpec((D_MODEL, tf), lambda i, f: (0, f)),
            pl.BlockSpec((D_MODEL, tf), lambda i, f: (0, f)),
            pl.BlockSpec((tf, D_MODEL), lambda i, f: (f, 0)),
            pl.BlockSpec((1, D_MODEL), lambda i, f: (0, 0)),
        ],
        out_specs=pl.BlockSpec((tm, D_MODEL), lambda i, f: (i, 0)),
        out_shape=jax.ShapeDtypeStruct((t, D_MODEL), F32),
        scratch_shapes=[pltpu.VMEM((tm, D_MODEL), F32)],
        compiler_params=pltpu.CompilerParams(
            dimension_semantics=("arbitrary", "arbitrary"),
            vmem_limit_bytes=VMEM_LIMIT),
        name="ffn",
    )(hn, hres, w_gate, w_up, w_down, norm_w)


def _lane_table(per_pair):
    return jnp.tile(per_pair.astype(F32), LANES // NPAIR).reshape(1, LANES)


def _trunk(x, p):
    b, l, _ = x.shape
    x2d = x.reshape(b * l, D_MODEL)
    proj, ba = _in_proj(x2d, p["norm_mix_w"], p["w_main"], p["w_ba"])
    proj3d = proj.reshape(b, l, MAIN_COLS)
    cols = _gates(ba.reshape(b, l, LANES), p["alog_lanes"], p["dtb_lanes"])
    y_na = _na(proj3d, p["rpb_flat"])
    y_dn = _gdn(proj3d, cols, p["conv_w"], p["dn_norm_w"])
    hres, hn = _out_proj(x2d, y_na.reshape(b * l, NA_WIDTH), y_dn.reshape(b * l, DN_WIDTH),
                         p["w_out_na"], p["w_out_dn"], p["norm_ffn_w"])
    y = _ffn(hn, hres, p["w_gate"], p["w_up"], p["w_down"], p["final_norm_w"])
    return y.reshape(b, l, D_MODEL)


def kernel(x_prompt, x_sample, norm_mix_w, w_in, na_rpb, dn_conv_w, dn_A_log, dn_dt_bias,
           dn_norm_w, w_out, norm_ffn_w, w_gate, w_up, w_down, final_norm_w):
    w_in0 = w_in[0]
    w_b = w_in0[:, OFF_B:OFF_A]
    w_a = w_in0[:, OFF_A:OFF_A + NPAIR]
    pad = jnp.zeros((D_MODEL, LANES - 6 * NPAIR), F32)
    p = {
        "norm_mix_w": norm_mix_w[0].reshape(1, D_MODEL),
        "w_main": w_in0[:, :MAIN_COLS].astype(BF16),
        "w_ba": jnp.concatenate([w_b, w_a, w_a, w_a, w_a, w_a, pad], axis=1).astype(BF16),
        "alog_lanes": _lane_table(dn_A_log[0].reshape(NPAIR)),
        "dtb_lanes": _lane_table(dn_dt_bias[0].reshape(NPAIR)),
        "rpb_flat": na_rpb[0].reshape(-1),
        "conv_w": dn_conv_w[0],
        "dn_norm_w": dn_norm_w[0].reshape(1, HEAD_DIM),
        "w_out_na": w_out[0, :NA_WIDTH].astype(BF16),
        "w_out_dn": w_out[0, NA_WIDTH:].astype(BF16),
        "norm_ffn_w": norm_ffn_w[0].reshape(1, D_MODEL),
        "w_gate": w_gate[0].astype(BF16),
        "w_up": w_up[0].astype(BF16),
        "w_down": w_down[0].astype(BF16),
        "final_norm_w": final_norm_w.reshape(1, D_MODEL),
    }
    return (_trunk(x_prompt, p), _trunk(x_sample, p))
```

```python
import functools

import jax
import jax.numpy as jnp
from jax import lax
from jax.experimental import pallas as pl
from jax.experimental.pallas import tpu as pltpu

D_MODEL = 2048
GRID_W = 64
HEAD_DIM = 128
NA_HEADS = 8
DN_HEADS = 8
NA_WIDTH = NA_HEADS * HEAD_DIM
DN_WIDTH = DN_HEADS * HEAD_DIM
NA_ROWS = 8
NA_COLS = 16
CONV_K = 5
CHUNK = 64
D_FF = 5632
RMS_EPS = 1e-6
L2_EPS = 1e-6

OFF_NA = 0
OFF_DN = OFF_NA + 3 * NA_WIDTH
OFF_Z = OFF_DN + 3 * DN_WIDTH
OFF_B = OFF_Z + DN_WIDTH
OFF_A = OFF_B + 2 * DN_HEADS
MAIN_COLS = OFF_B

LANES = 128
NPAIR = 2 * DN_HEADS
G_BETA, G_GHI, G_GLO, G_EG, G_ED, G_EL = range(6)
MASK_NEG = -1e30
VMEM_LIMIT = 56 * 1024 * 1024

F32 = jnp.float32
BF16 = jnp.bfloat16


def _silu(x):
    return x * (1.0 / (1.0 + jnp.exp(-x)))


def _nt(a, b):
    return lax.dot_general(a, b, (((1,), (1,)), ((), ())), preferred_element_type=F32)


def _nn(a, b):
    return jnp.dot(a, b, preferred_element_type=F32)


def _in_proj_kernel(x_ref, nw_ref, w_ref, wba_ref, proj_ref, ba_ref, xn_ref):
    @pl.when(pl.program_id(1) == 0)
    def _():
        x = x_ref[...]
        ms = jnp.mean(x * x, axis=-1, keepdims=True)
        xn = (x * lax.rsqrt(ms + RMS_EPS) * nw_ref[...]).astype(BF16)
        xn_ref[...] = xn
        ba_ref[...] = _nn(xn, wba_ref[...])

    proj_ref[...] = _nn(xn_ref[...], w_ref[...]).astype(BF16)


def _in_proj(x2d, norm_w, w_main, w_ba, tm=1024, tn=1024):
    t = x2d.shape[0]
    return pl.pallas_call(
        _in_proj_kernel,
        grid=(t // tm, MAIN_COLS // tn),
        in_specs=[
            pl.BlockSpec((tm, D_MODEL), lambda i, j: (i, 0)),
            pl.BlockSpec((1, D_MODEL), lambda i, j: (0, 0)),
            pl.BlockSpec((D_MODEL, tn), lambda i, j: (0, j)),
            pl.BlockSpec((D_MODEL, LANES), lambda i, j: (0, 0)),
        ],
        out_specs=[
            pl.BlockSpec((tm, tn), lambda i, j: (i, j)),
            pl.BlockSpec((tm, LANES), lambda i, j: (i, 0)),
        ],
        out_shape=[
            jax.ShapeDtypeStruct((t, MAIN_COLS), BF16),
            jax.ShapeDtypeStruct((t, LANES), F32),
        ],
        scratch_shapes=[pltpu.VMEM((tm, D_MODEL), BF16)],
        compiler_params=pltpu.CompilerParams(
            dimension_semantics=("arbitrary", "arbitrary"),
            vmem_limit_bytes=VMEM_LIMIT),
        name="in_proj",
    )(x2d, norm_w, w_main, w_ba)


def _gates_kernel(ba_ref, alog_ref, dtb_ref, out_ref):
    x = ba_ref[0]
    n = x.shape[0]
    lane = lax.broadcasted_iota(jnp.int32, x.shape, 1)
    pos = lax.broadcasted_iota(jnp.int32, x.shape, 0) % CHUNK
    grp = lane // NPAIR
    is_bwd = (lane % NPAIR) >= DN_HEADS

    beta = 1.0 / (1.0 + jnp.exp(-x))
    a = x + dtb_ref[...]
    softplus = jnp.maximum(a, 0.0) + jnp.log(1.0 + jnp.exp(-jnp.abs(a)))
    g = -jnp.exp(alog_ref[...]) * softplus

    pre = g
    suf = g
    s = 1
    while s < CHUNK:
        pre = pre + jnp.where(pos >= s, pltpu.roll(pre, s, axis=0), 0.0)
        suf = suf + jnp.where(pos < CHUNK - s, pltpu.roll(suf, n - s, axis=0), 0.0)
        s *= 2
    tot = pre + suf - g
    gc = jnp.where(is_bwd, suf, pre)
    ghi = gc.astype(BF16).astype(F32)

    out = jnp.where(grp == G_BETA, beta, 0.0)
    out = jnp.where(grp == G_GHI, ghi, out)
    out = jnp.where(grp == G_GLO, gc - ghi, out)
    out = jnp.where(grp == G_EG, jnp.exp(gc), out)
    out = jnp.where(grp == G_ED, jnp.exp(tot - gc), out)
    out = jnp.where(grp == G_EL, jnp.exp(tot), out)
    out_ref[0] = out.astype(BF16)


def _gates(ba3d, alog_lanes, dtb_lanes):
    b, l, _ = ba3d.shape
    return pl.pallas_call(
        _gates_kernel,
        grid=(b,),
        in_specs=[
            pl.BlockSpec((1, l, LANES), lambda i: (i, 0, 0)),
            pl.BlockSpec((1, LANES), lambda i: (0, 0)),
            pl.BlockSpec((1, LANES), lambda i: (0, 0)),
        ],
        out_specs=pl.BlockSpec((1, l, LANES), lambda i: (i, 0, 0)),
        out_shape=jax.ShapeDtypeStruct((b, l, LANES), BF16),
        compiler_params=pltpu.CompilerParams(
            dimension_semantics=("arbitrary",), vmem_limit_bytes=VMEM_LIMIT),
        name="gates",
    )(ba3d, alog_lanes, dtb_lanes)


N_DR = 2 * NA_ROWS - 1
N_DC = 2 * NA_COLS - 1
N_TAB = N_DR - 1
NA_ROW_G = 8


def _na_kernel(rpb_ref, q_ref, k_ref, v_ref, o_ref, tab_ref):
    h = pl.program_id(0)
    rows = q_ref.shape[1] // GRID_W
    band = NA_ROWS * GRID_W

    @pl.when(pl.program_id(1) == 0)
    def _():
        shape = (GRID_W, 2 * GRID_W)
        c = lax.broadcasted_iota(jnp.int32, shape, 0)
        lane = lax.broadcasted_iota(jnp.int32, shape, 1)
        kc = lane % GRID_W
        upper = lane >= GRID_W
        start = jnp.clip(c - NA_COLS // 2, 0, GRID_W - NA_COLS)
        valid = (kc >= start) & (kc < start + NA_COLS)
        delta = kc - c + (NA_COLS - 1)

        def build(d, carry):
            base = (h * N_DR + d) * N_DC
            val = jnp.full(shape, MASK_NEG, F32)
            for dc in range(N_DC):
                s0 = rpb_ref[base + dc]
                s1 = rpb_ref[base + N_DC + dc]
                val = jnp.where(delta == dc, jnp.where(upper, s1, s0), val)
            tab_ref[d] = jnp.where(valid, val, MASK_NEG)
            return carry

        lax.fori_loop(0, N_TAB, build, 0)

    scale = HEAD_DIM ** -0.5

    nparts = band // LANES

    def row_group(i, carry):
        rr = [i * NA_ROW_G + g for g in range(NA_ROW_G)]
        rs = [jnp.clip(r - NA_ROWS // 2, 0, rows - NA_ROWS) for r in rr]
        d0 = [s_ - r + (NA_ROWS - 1) for s_, r in zip(rs, rr)]
        qrow = [pl.ds(pl.multiple_of(r * GRID_W, GRID_W), GRID_W) for r in rr]
        brow = [pl.ds(pl.multiple_of(s_ * GRID_W, GRID_W), band) for s_ in rs]
        s = [_nt(q_ref[0, qr, :], k_ref[0, br, :]) for qr, br in zip(qrow, brow)]
        parts = [[x[:, j * LANES:(j + 1) * LANES] * scale + tab_ref[d + 2 * j]
                  for j in range(nparts)] for x, d in zip(s, d0)]
        m = [functools.reduce(jnp.maximum, ps_).max(axis=-1, keepdims=True) for ps_ in parts]
        e = [[jnp.exp(p_ - m_) for p_ in ps_] for ps_, m_ in zip(parts, m)]
        l = [functools.reduce(jnp.add, es_).sum(axis=-1, keepdims=True) for es_ in e]
        o = [_nn(jnp.concatenate(es_, axis=1).astype(BF16), v_ref[0, br, :])
             for es_, br in zip(e, brow)]
        for qr, o_, l_ in zip(qrow, o, l):
            o_ref[0, qr, :] = (o_ / l_).astype(BF16)
        return carry

    lax.fori_loop(0, rows // NA_ROW_G, row_group, 0)


def _na(proj3d, rpb_flat):
    b, l, _ = proj3d.shape
    blk = lambda off: pl.BlockSpec((1, l, HEAD_DIM), lambda h, i, off=off: (i, 0, off + h))
    return pl.pallas_call(
        _na_kernel,
        grid=(NA_HEADS, b),
        in_specs=[
            pl.BlockSpec(memory_space=pltpu.SMEM),
            blk(0), blk(NA_HEADS), blk(2 * NA_HEADS),
        ],
        out_specs=pl.BlockSpec((1, l, HEAD_DIM), lambda h, i: (i, 0, h)),
        out_shape=jax.ShapeDtypeStruct((b, l, NA_WIDTH), BF16),
        scratch_shapes=[pltpu.VMEM((N_TAB, GRID_W, 2 * GRID_W), F32)],
        compiler_params=pltpu.CompilerParams(
            dimension_semantics=("arbitrary", "arbitrary"),
            vmem_limit_bytes=VMEM_LIMIT),
        name="na",
    )(rpb_flat, proj3d, proj3d, proj3d)


SUB = 16


CONV_PAD = 8


CONV_TILE = 256


def _conv_silu(x_ref, cs, w, pad_ref, finish):
    n = x_ref.shape[1]
    half = CONV_K // 2

    def fill(t, carry):
        r0 = pl.multiple_of(t * CONV_TILE, CONV_TILE)
        pad_ref[pl.ds(CONV_PAD + r0, CONV_TILE), :] = (
            x_ref[0, pl.ds(r0, CONV_TILE), cs].astype(F32))
        return carry

    def tile(t, carry):
        r0 = pl.multiple_of(t * CONV_TILE, CONV_TILE)
        acc = None
        for i in range(CONV_K):
            term = pad_ref[pl.ds(r0 + (CONV_PAD + i - half), CONV_TILE), :] * w[i:i + 1, :]
            acc = term if acc is None else acc + term
        finish(r0, _silu(acc))
        return carry

    lax.fori_loop(0, n // CONV_TILE, fill, 0)
    lax.fori_loop(0, n // CONV_TILE, tile, 0, unroll=4)


def _l2norm(x):
    return x * lax.rsqrt(jnp.sum(x * x, axis=-1, keepdims=True) + L2_EPS)


def _par(f, *lists):
    return [f(*xs) for xs in zip(*lists)]


def _unit_triangular_inverses(mats):
    n = mats[0].shape[0]
    i = lax.broadcasted_iota(jnp.int32, (n, n), 0)
    j = lax.broadcasted_iota(jnp.int32, (n, n), 1)
    same = (i // SUB) == (j // SUB)
    eye = jnp.where(i == j, 1.0, 0.0)
    bf = lambda x: x.astype(BF16)
    d = _par(lambda a: jnp.where(same, a, 0.0), mats)
    rest = _par(lambda a, dd: bf(a - dd), mats, d)
    dp = _par(bf, d)
    p = _par(lambda dd: eye - dd, d)
    step = 1
    while 2 * step < SUB:
        dp = _par(lambda x: bf(_nn(x, x)), dp)
        yield
        p = _par(lambda pp, x: pp + _nn(bf(pp), x), p, dp)
        yield
        step *= 2
    m = _par(lambda pp, rr: _nn(bf(pp), rr), p, rest)
    yield
    m2 = _par(lambda mm: bf(_nn(bf(mm), bf(mm))), m)
    yield
    r = _par(lambda mm, sq: _nn(bf(eye - mm), sq) - mm, m, m2)
    yield
    return _par(lambda pp, rr: pp + _nn(bf(rr), bf(pp)), p, r)


def _run_staged(stages, extras):
    extras = list(extras)
    for k, _ in enumerate(stages):
        if extras and k % 2 == 0:
            extras.pop(0)()
    for f in extras:
        f()


GDN_HG = 2
GDN_PREP_G = 8
GDN_CHAINS = 2 * GDN_HG
GDN_W = GDN_HG * HEAD_DIM
MQ_ROWS = HEAD_DIM + CHUNK
BCAST_GROUPS = (G_BETA, G_EG, G_ED)
N_BC = len(BCAST_GROUPS)


def _gdn_kernel(q_ref, k_ref, v_ref, z_ref, cols_ref, wq_ref, wk_ref, wv_ref, nw_ref,
                o_ref, pad_s, qs, ks, vs, esel_s, mq_s, co_s, el_s, st_s, oacc):
    hg = pl.program_id(1)
    l = q_ref.shape[1]
    nchunk = l // CHUNK
    heads = [slice(hh * HEAD_DIM, (hh + 1) * HEAD_DIM) for hh in range(GDN_HG)]

    zero_pad = jnp.zeros((CONV_PAD, HEAD_DIM), F32)
    pad_s[:CONV_PAD, :] = zero_pad
    pad_s[CONV_PAD + l:, :] = zero_pad
    for cs in heads:
        def put_q(r0, y, cs=cs):
            qs[pl.ds(r0, CONV_TILE), cs] = (_l2norm(y) * (HEAD_DIM ** -0.5)).astype(BF16)

        def put_k(r0, y, cs=cs):
            ks[pl.ds(r0, CONV_TILE), cs] = _l2norm(y).astype(BF16)

        def put_v(r0, y, cs=cs):
            vs[pl.ds(r0, CONV_TILE), cs] = y.astype(BF16)

        _conv_silu(q_ref, cs, wq_ref[:, cs], pad_s, put_q)
        _conv_silu(k_ref, cs, wk_ref[:, cs], pad_s, put_k)
        _conv_silu(v_ref, cs, wv_ref[:, cs], pad_s, put_v)

    bc_shape = (LANES, N_BC * LANES)
    bc_row = lax.broadcasted_iota(jnp.int32, bc_shape, 0)
    bc_blk = lax.broadcasted_iota(jnp.int32, bc_shape, 1) // LANES
    bc_grp = jnp.zeros(bc_shape, jnp.int32)
    for i, g in enumerate(BCAST_GROUPS):
        bc_grp = jnp.where(bc_blk == i, g, bc_grp)
    sel_r = lax.broadcasted_iota(jnp.int32, (LANES, LANES), 0)
    for idx in range(GDN_CHAINS):
        pair = (idx % 2) * DN_HEADS + hg * GDN_HG + idx // 2
        hi, lo = G_GHI * NPAIR + pair, G_GLO * NPAIR + pair
        esel_s[idx, :, :N_BC * LANES] = jnp.where(
            bc_row == bc_grp * NPAIR + pair, 1.0, 0.0).astype(BF16)
        esel_s[idx, :, N_BC * LANES:(N_BC + 1) * LANES] = jnp.where(
            (sel_r == hi) | (sel_r == lo), 1.0, 0.0).astype(BF16)
        esel_s[idx, :, (N_BC + 1) * LANES:] = jnp.where(
            sel_r == G_EL * NPAIR + pair, 1.0, 0.0).astype(BF16)
        st_s[idx] = jnp.zeros((HEAD_DIM, HEAD_DIM), F32)

    ri = lax.broadcasted_iota(jnp.int32, (CHUNK, CHUNK), 0)
    ci = lax.broadcasted_iota(jnp.int32, (CHUNK, CHUNK), 1)

    bf = lambda x: x.astype(BF16)
    tri = ((ri >= ci, ri > ci), (ri <= ci, ri < ci))

    nblock = nchunk // GDN_PREP_G
    chains = [(g, hh, d) for g in range(GDN_PREP_G) for hh in range(GDN_HG) for d in range(2)]
    chain_idx = [2 * hh + d for _, hh, d in chains]
    chain_dir = [d for _, _, d in chains]

    def prep_stages(i):
        step = [i * GDN_PREP_G + g for g, _, _ in chains]
        chunk = [s if d == 0 else nchunk - 1 - s for s, d in zip(step, chain_dir)]
        rows = [pl.ds(pl.multiple_of(n * CHUNK, CHUNK), CHUNK) for n in chunk]
        kc = [ks[r, heads[hh]] for r, (_, hh, _) in zip(rows, chains)]
        qc = [qs[r, heads[hh]] for r, (_, hh, _) in zip(rows, chains)]
        vf = [vs[r, heads[hh]].astype(F32) for r, (_, hh, _) in zip(rows, chains)]
        cols = [cols_ref[0, r, :] for r in rows]
        kf = _par(lambda x: x.astype(F32), kc)
        qf = _par(lambda x: x.astype(F32), qc)
        qkk = _par(lambda k_, q_: _nt(jnp.concatenate([k_, q_], axis=0), k_), kc, qc)
        bc = [_nn(c_, esel_s[idx, :, :N_BC * LANES]) for c_, idx in zip(cols, chain_idx)]
        g_bc = [_nn(c_, esel_s[idx, :, N_BC * LANES:(N_BC + 1) * LANES])
                for c_, idx in zip(cols, chain_idx)]
        el = [_nn(c_[:16, :], esel_s[idx, :, (N_BC + 1) * LANES:])[:8, :]
              for c_, idx in zip(cols, chain_idx)]
        yield
        g_col = [x[:, :CHUNK] for x in g_bc]
        g_row = [x.T[:CHUNK, :] for x in g_bc]
        beta_b, eg_b, ed_b = (
            [x[:, i_ * LANES:(i_ + 1) * LANES] for x in bc] for i_ in range(N_BC))
        decay = [jnp.where(tri[d][0], jnp.exp(jnp.where(tri[d][0], gc_ - gr_, 0.0)), 0.0)
                 for d, gc_, gr_ in zip(chain_dir, g_col, g_row)]
        a = [jnp.where(tri[d][1], x[:CHUNK] * be[:, :CHUNK] * dc, 0.0)
             for d, x, be, dc in zip(chain_dir, qkk, beta_b, decay)]
        tmat = yield from _unit_triangular_inverses(a)
        rhs = [jnp.concatenate([bf(k_ * (be * eg)), bf(v_ * be)], axis=1)
               for k_, v_, be, eg in zip(kf, vf, beta_b, eg_b)]
        wu = _par(lambda t_, r_: bf(_nn(bf(t_), r_)), tmat, rhs)
        yield
        kdt = [bf((k_ * ed).T) for k_, ed in zip(kf, ed_b)]
        qkm = [bf(jnp.where(tri[d][0], x[CHUNK:] * dc, 0.0))
               for d, x, dc in zip(chain_dir, qkk, decay)]
        res = _par(lambda kt, qm, w_: _nn(jnp.concatenate([kt, qm], axis=0), w_),
                   kdt, qkm, wu)
        yield
        for idx, n, rs, q_, eg, el_ in zip(chain_idx, chunk, res, qf, eg_b, el):
            mq_s[idx, n, :HEAD_DIM] = bf(-rs[:HEAD_DIM, :HEAD_DIM])
            mq_s[idx, n, HEAD_DIM:] = bf(q_ * eg - rs[HEAD_DIM:, :HEAD_DIM])
            co_s[idx, n] = rs[:, HEAD_DIM:]
            el_s[idx, n] = el_

    def scan_step(s):
        for idx in range(GDN_CHAINS):
            n = s if idx % 2 == 0 else nchunk - 1 - s
            st = st_s[idx]
            x = co_s[idx, n] + _nn(mq_s[idx, n], st.astype(BF16))
            oacc[idx, pl.ds(pl.multiple_of(n * CHUNK, CHUNK), CHUNK), :] = x[HEAD_DIM:]
            el = jnp.broadcast_to(el_s[idx, n][:1, :], (HEAD_DIM, HEAD_DIM))
            st_s[idx] = st * el + x[:HEAD_DIM]

    def first_block(i, carry):
        _run_staged(prep_stages(i), [])
        return carry

    def block(i, carry):
        steps = [functools.partial(scan_step, (i - 1) * GDN_PREP_G + g)
                 for g in range(GDN_PREP_G)]
        _run_staged(prep_stages(i), steps)
        return carry

    def last_steps(s, carry):
        scan_step(s)
        return carry

    lax.fori_loop(0, 1, first_block, 0)
    lax.fori_loop(1, nblock, block, 0)
    lax.fori_loop(nchunk - GDN_PREP_G, nchunk, last_steps, 0)

    for hh, cs in enumerate(heads):
        o = oacc[2 * hh] + oacc[2 * hh + 1]
        o = o * lax.rsqrt(jnp.mean(o * o, axis=-1, keepdims=True) + RMS_EPS) * nw_ref[...]
        o_ref[0, :, cs] = (o * _silu(z_ref[0, :, cs].astype(F32))).astype(BF16)


def _gdn(proj3d, cols, conv_w, norm_w):
    b, l, _ = proj3d.shape
    nchunk = l // CHUNK
    assert DN_HEADS % GDN_HG == 0 and nchunk % GDN_PREP_G == 0
    groups = DN_HEADS // GDN_HG
    blk = lambda off: pl.BlockSpec((1, l, GDN_W), lambda i, h, off=off: (i, 0, off // GDN_W + h))
    wblk = lambda off: pl.BlockSpec((CONV_K, GDN_W), lambda i, h, off=off: (0, off // GDN_W + h))
    return pl.pallas_call(
        _gdn_kernel,
        grid=(b, groups),
        in_specs=[
            blk(OFF_DN), blk(OFF_DN + DN_WIDTH), blk(OFF_DN + 2 * DN_WIDTH), blk(OFF_Z),
            pl.BlockSpec((1, l, LANES), lambda i, h: (i, 0, 0)),
            wblk(0), wblk(DN_WIDTH), wblk(2 * DN_WIDTH),
            pl.BlockSpec((1, HEAD_DIM), lambda i, h: (0, 0)),
        ],
        out_specs=pl.BlockSpec((1, l, GDN_W), lambda i, h: (i, 0, h)),
        out_shape=jax.ShapeDtypeStruct((b, l, DN_WIDTH), BF16),
        scratch_shapes=[
            pltpu.VMEM((l + 2 * CONV_PAD, HEAD_DIM), F32),
            pltpu.VMEM((l, GDN_W), BF16),
            pltpu.VMEM((l, GDN_W), BF16),
            pltpu.VMEM((l, GDN_W), BF16),
            pltpu.VMEM((GDN_CHAINS, LANES, (N_BC + 2) * LANES), BF16),
            pltpu.VMEM((GDN_CHAINS, nchunk, MQ_ROWS, HEAD_DIM), BF16),
            pltpu.VMEM((GDN_CHAINS, nchunk, MQ_ROWS, HEAD_DIM), F32),
            pltpu.VMEM((GDN_CHAINS, nchunk, 8, HEAD_DIM), F32),
            pltpu.VMEM((GDN_CHAINS, HEAD_DIM, HEAD_DIM), F32),
            pltpu.VMEM((GDN_CHAINS, l, HEAD_DIM), F32),
        ],
        compiler_params=pltpu.CompilerParams(
            dimension_semantics=("arbitrary", "arbitrary"),
            vmem_limit_bytes=VMEM_LIMIT),
        name="gdn",
    )(proj3d, proj3d, proj3d, proj3d, cols, conv_w, conv_w, conv_w, norm_w)


def _out_proj_kernel(x_ref, yna_ref, ydn_ref, wa_ref, wb_ref, nw_ref, h_ref, hn_ref):
    mixed = _nn(yna_ref[...], wa_ref[...]) + _nn(ydn_ref[...], wb_ref[...])
    hres = x_ref[...] + mixed
    h_ref[...] = hres
    ms = jnp.mean(hres * hres, axis=-1, keepdims=True)
    hn_ref[...] = (hres * lax.rsqrt(ms + RMS_EPS) * nw_ref[...]).astype(BF16)


def _out_proj(x2d, yna, ydn, w_a, w_b, norm_w, tm=512):
    t = x2d.shape[0]
    row = lambda w: pl.BlockSpec((tm, w), lambda i: (i, 0))
    full = lambda a: pl.BlockSpec(a.shape, lambda i: (0, 0))
    return pl.pallas_call(
        _out_proj_kernel,
        grid=(t // tm,),
        in_specs=[row(D_MODEL), row(NA_WIDTH), row(DN_WIDTH), full(w_a), full(w_b),
                  full(norm_w)],
        out_specs=[row(D_MODEL), row(D_MODEL)],
        out_shape=[jax.ShapeDtypeStruct((t, D_MODEL), F32),
                   jax.ShapeDtypeStruct((t, D_MODEL), BF16)],
        compiler_params=pltpu.CompilerParams(
            dimension_semantics=("arbitrary",), vmem_limit_bytes=VMEM_LIMIT),
        name="out_proj",
    )(x2d, yna, ydn, w_a, w_b, norm_w)


def _ffn_kernel(hn_ref, h_ref, wg_ref, wu_ref, wd_ref, nw_ref, o_ref, acc_ref):
    f = pl.program_id(1)

    @pl.when(f == 0)
    def _():
        acc_ref[...] = h_ref[...]

    hn = hn_ref[...]
    g = _nn(hn, wg_ref[...])
    u = _nn(hn, wu_ref[...])
    acc_ref[...] += _nn((_silu(g) * u).astype(BF16), wd_ref[...])

    @pl.when(f == pl.num_programs(1) - 1)
    def _():
        y = acc_ref[...]
        ms = jnp.mean(y * y, axis=-1, keepdims=True)
        o_ref[...] = y * lax.rsqrt(ms + RMS_EPS) * nw_ref[...]


def _ffn(hn, hres, w_gate, w_up, w_down, norm_w, tm=512, tf=512):
    t = hn.shape[0]
    return pl.pallas_call(
        _ffn_kernel,
        grid=(t // tm, D_FF // tf),
        in_specs=[
            pl.BlockSpec((tm, D_MODEL), lambda i, f: (i, 0)),
            pl.BlockSpec((tm, D_MODEL), lambda i, f: (i, 0)),
            pl.BlockSpec((D_MODEL, tf), lambda i, f: (0, f)),
            pl.BlockSpec((D_MODEL, tf), lambda i, f: (0, f)),
            pl.BlockSpec((tf, D_MODEL), lambda i, f: (f, 0)),
            pl.BlockSpec((1, D_MODEL), lambda i, f: (0, 0)),
        ],
        out_specs=pl.BlockSpec((tm, D_MODEL), lambda i, f: (i, 0)),
        out_shape=jax.ShapeDtypeStruct((t, D_MODEL), F32),
        scratch_shapes=[pltpu.VMEM((tm, D_MODEL), F32)],
        compiler_params=pltpu.CompilerParams(
            dimension_semantics=("arbitrary", "arbitrary"),
            vmem_limit_bytes=VMEM_LIMIT),
        name="ffn",
    )(hn, hres, w_gate, w_up, w_down, norm_w)


def _lane_table(per_pair):
    return jnp.tile(per_pair.astype(F32), LANES // NPAIR).reshape(1, LANES)


def _trunk(x, p):
    b, l, _ = x.shape
    x2d = x.reshape(b * l, D_MODEL)
    proj, ba = _in_proj(x2d, p["norm_mix_w"], p["w_main"], p["w_ba"])
    proj3d = proj.reshape(b, l, MAIN_COLS)
    cols = _gates(ba.reshape(b, l, LANES), p["alog_lanes"], p["dtb_lanes"])
    y_na = _na(proj3d, p["rpb_flat"])
    y_dn = _gdn(proj3d, cols, p["conv_w"], p["dn_norm_w"])
    hres, hn = _out_proj(x2d, y_na.reshape(b * l, NA_WIDTH), y_dn.reshape(b * l, DN_WIDTH),
                         p["w_out_na"], p["w_out_dn"], p["norm_ffn_w"])
    y = _ffn(hn, hres, p["w_gate"], p["w_up"], p["w_down"], p["final_norm_w"])
    return y.reshape(b, l, D_MODEL)


def kernel(x_prompt, x_sample, norm_mix_w, w_in, na_rpb, dn_conv_w, dn_A_log, dn_dt_bias,
           dn_norm_w, w_out, norm_ffn_w, w_gate, w_up, w_down, final_norm_w):
    w_in0 = w_in[0]
    w_b = w_in0[:, OFF_B:OFF_A]
    w_a = w_in0[:, OFF_A:OFF_A + NPAIR]
    pad = jnp.zeros((D_MODEL, LANES - 6 * NPAIR), F32)
    p = {
        "norm_mix_w": norm_mix_w[0].reshape(1, D_MODEL),
        "w_main": w_in0.astype(BF16),
        "w_ba": jnp.concatenate([w_b, w_a, w_a, w_a, w_a, w_a, pad], axis=1).astype(BF16),
        "alog_lanes": _lane_table(dn_A_log[0].reshape(NPAIR)),
        "dtb_lanes": _lane_table(dn_dt_bias[0].reshape(NPAIR)),
        "rpb_flat": na_rpb[0].reshape(-1),
        "conv_w": dn_conv_w[0],
        "dn_norm_w": dn_norm_w[0].reshape(1, HEAD_DIM),
        "w_out_na": w_out[0, :NA_WIDTH].astype(BF16),
        "w_out_dn": w_out[0, NA_WIDTH:].astype(BF16),
        "norm_ffn_w": norm_ffn_w[0].reshape(1, D_MODEL),
        "w_gate": w_gate[0].astype(BF16),
        "w_up": w_up[0].astype(BF16),
        "w_down": w_down[0].astype(BF16),
        "final_norm_w": final_norm_w.reshape(1, D_MODEL),
    }
    return (_trunk(x_prompt, p), _trunk(x_sample, p))
```

```python
import functools

import jax
import jax.numpy as jnp
from jax import lax
from jax.experimental import pallas as pl
from jax.experimental.pallas import tpu as pltpu

D_MODEL = 2048
GRID_W = 64
HEAD_DIM = 128
NA_HEADS = 8
DN_HEADS = 8
NA_WIDTH = NA_HEADS * HEAD_DIM
DN_WIDTH = DN_HEADS * HEAD_DIM
NA_ROWS = 8
NA_COLS = 16
CONV_K = 5
CHUNK = 64
D_FF = 5632
RMS_EPS = 1e-6
L2_EPS = 1e-6

OFF_NA = 0
OFF_DN = OFF_NA + 3 * NA_WIDTH
OFF_Z = OFF_DN + 3 * DN_WIDTH
OFF_B = OFF_Z + DN_WIDTH
OFF_A = OFF_B + 2 * DN_HEADS
MAIN_COLS = OFF_B

LANES = 128
NPAIR = 2 * DN_HEADS
G_BETA, G_GHI, G_GLO, G_EG, G_ED = range(5)
N_GATE_GROUPS = 5
MASK_NEG = -1e30
VMEM_LIMIT = 56 * 1024 * 1024

F32 = jnp.float32
BF16 = jnp.bfloat16


def _silu(x):
    return x * (1.0 / (1.0 + jnp.exp(-x)))


def _nt(a, b):
    return lax.dot_general(a, b, (((1,), (1,)), ((), ())), preferred_element_type=F32)


def _nn(a, b):
    return jnp.dot(a, b, preferred_element_type=F32)


def _in_proj_kernel(x_ref, nw_ref, w_ref, wba_ref, proj_ref, ba_ref, xn_ref):
    @pl.when(pl.program_id(1) == 0)
    def _():
        x = x_ref[...]
        ms = jnp.mean(x * x, axis=-1, keepdims=True)
        xn = (x * lax.rsqrt(ms + RMS_EPS) * nw_ref[...]).astype(BF16)
        xn_ref[...] = xn
        ba_ref[...] = _nn(xn, wba_ref[...])

    proj_ref[...] = _nn(xn_ref[...], w_ref[...]).astype(BF16)


def _in_proj(x2d, norm_w, w_main, w_ba, tm=1024, tn=1024):
    t = x2d.shape[0]
    return pl.pallas_call(
        _in_proj_kernel,
        grid=(t // tm, MAIN_COLS // tn),
        in_specs=[
            pl.BlockSpec((tm, D_MODEL), lambda i, j: (i, 0)),
            pl.BlockSpec((1, D_MODEL), lambda i, j: (0, 0)),
            pl.BlockSpec((D_MODEL, tn), lambda i, j: (0, j)),
            pl.BlockSpec((D_MODEL, LANES), lambda i, j: (0, 0)),
        ],
        out_specs=[
            pl.BlockSpec((tm, tn), lambda i, j: (i, j)),
            pl.BlockSpec((tm, LANES), lambda i, j: (i, 0)),
        ],
        out_shape=[
            jax.ShapeDtypeStruct((t, MAIN_COLS), BF16),
            jax.ShapeDtypeStruct((t, LANES), F32),
        ],
        scratch_shapes=[pltpu.VMEM((tm, D_MODEL), BF16)],
        compiler_params=pltpu.CompilerParams(
            dimension_semantics=("arbitrary", "arbitrary"),
            vmem_limit_bytes=VMEM_LIMIT),
        name="in_proj",
    )(x2d, norm_w, w_main, w_ba)


def _gates_kernel(ba_ref, alog_ref, dtb_ref, out_ref):
    x = ba_ref[0]
    n = x.shape[0]
    lane = lax.broadcasted_iota(jnp.int32, x.shape, 1)
    pos = lax.broadcasted_iota(jnp.int32, x.shape, 0) % CHUNK
    grp = lane // NPAIR
    is_bwd = (lane % NPAIR) >= DN_HEADS

    beta = 1.0 / (1.0 + jnp.exp(-x))
    a = x + dtb_ref[...]
    softplus = jnp.maximum(a, 0.0) + jnp.log(1.0 + jnp.exp(-jnp.abs(a)))
    g = -jnp.exp(alog_ref[...]) * softplus

    pre = g
    suf = g
    s = 1
    while s < CHUNK:
        pre = pre + jnp.where(pos >= s, pltpu.roll(pre, s, axis=0), 0.0)
        suf = suf + jnp.where(pos < CHUNK - s, pltpu.roll(suf, n - s, axis=0), 0.0)
        s *= 2
    tot = pre + suf - g
    gc = jnp.where(is_bwd, suf, pre)
    ghi = gc.astype(BF16).astype(F32)

    out = jnp.where(grp == G_BETA, beta, 0.0)
    out = jnp.where(grp == G_GHI, ghi, out)
    out = jnp.where(grp == G_GLO, gc - ghi, out)
    out = jnp.where(grp == G_EG, jnp.exp(gc), out)
    out = jnp.where(grp == G_ED, jnp.exp(tot - gc), out)
    out_ref[0] = out.astype(BF16)


def _gates(ba3d, alog_lanes, dtb_lanes):
    b, l, _ = ba3d.shape
    return pl.pallas_call(
        _gates_kernel,
        grid=(b,),
        in_specs=[
            pl.BlockSpec((1, l, LANES), lambda i: (i, 0, 0)),
            pl.BlockSpec((1, LANES), lambda i: (0, 0)),
            pl.BlockSpec((1, LANES), lambda i: (0, 0)),
        ],
        out_specs=pl.BlockSpec((1, l, LANES), lambda i: (i, 0, 0)),
        out_shape=jax.ShapeDtypeStruct((b, l, LANES), BF16),
        compiler_params=pltpu.CompilerParams(
            dimension_semantics=("arbitrary",), vmem_limit_bytes=VMEM_LIMIT),
        name="gates",
    )(ba3d, alog_lanes, dtb_lanes)


N_DR = 2 * NA_ROWS - 1
N_DC = 2 * NA_COLS - 1
N_TAB = N_DR - 1
NA_ROW_G = 8


def _na_kernel(rpb_ref, q_ref, k_ref, v_ref, o_ref, tab_ref):
    h = pl.program_id(0)
    rows = q_ref.shape[1] // GRID_W
    band = NA_ROWS * GRID_W

    @pl.when(pl.program_id(1) == 0)
    def _():
        shape = (GRID_W, 2 * GRID_W)
        c = lax.broadcasted_iota(jnp.int32, shape, 0)
        lane = lax.broadcasted_iota(jnp.int32, shape, 1)
        kc = lane % GRID_W
        upper = lane >= GRID_W
        start = jnp.clip(c - NA_COLS // 2, 0, GRID_W - NA_COLS)
        valid = (kc >= start) & (kc < start + NA_COLS)
        delta = kc - c + (NA_COLS - 1)

        def build(d, carry):
            base = (h * N_DR + d) * N_DC
            val = jnp.full(shape, MASK_NEG, F32)
            for dc in range(N_DC):
                s0 = rpb_ref[base + dc]
                s1 = rpb_ref[base + N_DC + dc]
                val = jnp.where(delta == dc, jnp.where(upper, s1, s0), val)
            tab_ref[d] = jnp.where(valid, val, MASK_NEG)
            return carry

        lax.fori_loop(0, N_TAB, build, 0)

    scale = HEAD_DIM ** -0.5

    nparts = band // LANES

    def row_group(i, carry):
        rr = [i * NA_ROW_G + g for g in range(NA_ROW_G)]
        rs = [jnp.clip(r - NA_ROWS // 2, 0, rows - NA_ROWS) for r in rr]
        d0 = [s_ - r + (NA_ROWS - 1) for s_, r in zip(rs, rr)]
        qrow = [pl.ds(pl.multiple_of(r * GRID_W, GRID_W), GRID_W) for r in rr]
        brow = [pl.ds(pl.multiple_of(s_ * GRID_W, GRID_W), band) for s_ in rs]
        s = [_nt(q_ref[0, qr, :], k_ref[0, br, :]) for qr, br in zip(qrow, brow)]
        parts = [[x[:, j * LANES:(j + 1) * LANES] * scale + tab_ref[d + 2 * j]
                  for j in range(nparts)] for x, d in zip(s, d0)]
        m = [functools.reduce(jnp.maximum, ps_).max(axis=-1, keepdims=True) for ps_ in parts]
        e = [[jnp.exp(p_ - m_) for p_ in ps_] for ps_, m_ in zip(parts, m)]
        l = [functools.reduce(jnp.add, es_).sum(axis=-1, keepdims=True) for es_ in e]
        o = [_nn(jnp.concatenate(es_, axis=1).astype(BF16), v_ref[0, br, :])
             for es_, br in zip(e, brow)]
        for qr, o_, l_ in zip(qrow, o, l):
            o_ref[0, qr, :] = (o_ / l_).astype(BF16)
        return carry

    lax.fori_loop(0, rows // NA_ROW_G, row_group, 0)


def _na(proj3d, rpb_flat):
    b, l, _ = proj3d.shape
    blk = lambda off: pl.BlockSpec((1, l, HEAD_DIM), lambda h, i, off=off: (i, 0, off + h))
    return pl.pallas_call(
        _na_kernel,
        grid=(NA_HEADS, b),
        in_specs=[
            pl.BlockSpec(memory_space=pltpu.SMEM),
            blk(0), blk(NA_HEADS), blk(2 * NA_HEADS),
        ],
        out_specs=pl.BlockSpec((1, l, HEAD_DIM), lambda h, i: (i, 0, h)),
        out_shape=jax.ShapeDtypeStruct((b, l, NA_WIDTH), BF16),
        scratch_shapes=[pltpu.VMEM((N_TAB, GRID_W, 2 * GRID_W), F32)],
        compiler_params=pltpu.CompilerParams(
            dimension_semantics=("arbitrary", "arbitrary"),
            vmem_limit_bytes=VMEM_LIMIT),
        name="na",
    )(rpb_flat, proj3d, proj3d, proj3d)


SUB = 16


CONV_PAD = 8


CONV_TILE = 256


def _conv_silu(x_ref, cs, w, pad_ref, finish):
    n = x_ref.shape[1]
    half = CONV_K // 2

    def fill(t, carry):
        r0 = pl.multiple_of(t * CONV_TILE, CONV_TILE)
        pad_ref[pl.ds(CONV_PAD + r0, CONV_TILE), :] = (
            x_ref[0, pl.ds(r0, CONV_TILE), cs].astype(F32))
        return carry

    def tile(t, carry):
        r0 = pl.multiple_of(t * CONV_TILE, CONV_TILE)
        acc = None
        for i in range(CONV_K):
            term = pad_ref[pl.ds(r0 + (CONV_PAD + i - half), CONV_TILE), :] * w[i:i + 1, :]
            acc = term if acc is None else acc + term
        finish(r0, _silu(acc))
        return carry

    lax.fori_loop(0, n // CONV_TILE, fill, 0)
    lax.fori_loop(0, n // CONV_TILE, tile, 0, unroll=4)


def _l2norm(x):
    return x * lax.rsqrt(jnp.sum(x * x, axis=-1, keepdims=True) + L2_EPS)


def _par(f, *lists):
    return [f(*xs) for xs in zip(*lists)]


def _unit_triangular_inverses(mats):
    n = mats[0].shape[0]
    i = lax.broadcasted_iota(jnp.int32, (n, n), 0)
    j = lax.broadcasted_iota(jnp.int32, (n, n), 1)
    same = (i // SUB) == (j // SUB)
    eye = jnp.where(i == j, 1.0, 0.0)
    bf = lambda x: x.astype(BF16)
    d = _par(lambda a: jnp.where(same, a, 0.0), mats)
    rest = _par(lambda a, dd: bf(a - dd), mats, d)
    dp = _par(bf, d)
    p = _par(lambda dd: eye - dd, d)
    step = 1
    while 2 * step < SUB:
        dp = _par(lambda x: bf(_nn(x, x)), dp)
        yield
        p = _par(lambda pp, x: pp + _nn(bf(pp), x), p, dp)
        yield
        step *= 2
    m = _par(lambda pp, rr: _nn(bf(pp), rr), p, rest)
    yield
    m2 = _par(lambda mm: bf(_nn(bf(mm), bf(mm))), m)
    yield
    r = _par(lambda mm, sq: _nn(bf(eye - mm), sq) - mm, m, m2)
    yield
    return _par(lambda pp, rr: pp + _nn(bf(rr), bf(pp)), p, r)


def _run_staged(stages, extras):
    extras = list(extras)
    for k, _ in enumerate(stages):
        if extras and k % 2 == 0:
            extras.pop(0)()
    for f in extras:
        f()


GDN_HG = 2
GDN_PREP_G = 8
GDN_CHAINS = 2 * GDN_HG
GDN_W = GDN_HG * HEAD_DIM
MQ_ROWS = HEAD_DIM + CHUNK
BCAST_GROUPS = (G_BETA, G_EG, G_ED)
N_BC = len(BCAST_GROUPS)


def _gdn_kernel(q_ref, k_ref, v_ref, z_ref, cols_ref, wq_ref, wk_ref, wv_ref, nw_ref,
                o_ref, pad_s, qs, ks, vs, esel_s, mq_s, co_s, el_s, st_s, oacc):
    hg = pl.program_id(1)
    l = q_ref.shape[1]
    nchunk = l // CHUNK
    heads = [slice(hh * HEAD_DIM, (hh + 1) * HEAD_DIM) for hh in range(GDN_HG)]

    zero_pad = jnp.zeros((CONV_PAD, HEAD_DIM), F32)
    pad_s[:CONV_PAD, :] = zero_pad
    pad_s[CONV_PAD + l:, :] = zero_pad
    for cs in heads:
        def put_q(r0, y, cs=cs):
            qs[pl.ds(r0, CONV_TILE), cs] = (_l2norm(y) * (HEAD_DIM ** -0.5)).astype(BF16)

        def put_k(r0, y, cs=cs):
            ks[pl.ds(r0, CONV_TILE), cs] = _l2norm(y).astype(BF16)

        def put_v(r0, y, cs=cs):
            vs[pl.ds(r0, CONV_TILE), cs] = y.astype(BF16)

        _conv_silu(q_ref, cs, wq_ref[:, cs], pad_s, put_q)
        _conv_silu(k_ref, cs, wk_ref[:, cs], pad_s, put_k)
        _conv_silu(v_ref, cs, wv_ref[:, cs], pad_s, put_v)

    bc_shape = (LANES, N_BC * LANES)
    bc_row = lax.broadcasted_iota(jnp.int32, bc_shape, 0)
    bc_blk = lax.broadcasted_iota(jnp.int32, bc_shape, 1) // LANES
    bc_grp = jnp.zeros(bc_shape, jnp.int32)
    for i, g in enumerate(BCAST_GROUPS):
        bc_grp = jnp.where(bc_blk == i, g, bc_grp)
    sel_r = lax.broadcasted_iota(jnp.int32, (LANES, LANES), 0)
    for idx in range(GDN_CHAINS):
        pair = (idx % 2) * DN_HEADS + hg * GDN_HG + idx // 2
        hi, lo = G_GHI * NPAIR + pair, G_GLO * NPAIR + pair
        esel_s[idx, :, :N_BC * LANES] = jnp.where(
            bc_row == bc_grp * NPAIR + pair, 1.0, 0.0).astype(BF16)
        esel_s[idx, :, N_BC * LANES:] = jnp.where(
            (sel_r == hi) | (sel_r == lo), 1.0, 0.0).astype(BF16)
        st_s[idx] = jnp.zeros((HEAD_DIM, HEAD_DIM), F32)

    ri = lax.broadcasted_iota(jnp.int32, (CHUNK, CHUNK), 0)
    ci = lax.broadcasted_iota(jnp.int32, (CHUNK, CHUNK), 1)

    bf = lambda x: x.astype(BF16)
    tri = ((ri >= ci, ri > ci), (ri <= ci, ri < ci))

    nblock = nchunk // GDN_PREP_G
    chains = [(g, hh, d) for g in range(GDN_PREP_G) for hh in range(GDN_HG) for d in range(2)]
    chain_idx = [2 * hh + d for _, hh, d in chains]
    chain_dir = [d for _, _, d in chains]

    def prep_stages(i):
        step = [i * GDN_PREP_G + g for g, _, _ in chains]
        chunk = [s if d == 0 else nchunk - 1 - s for s, d in zip(step, chain_dir)]
        rows = [pl.ds(pl.multiple_of(n * CHUNK, CHUNK), CHUNK) for n in chunk]
        kc = [ks[r, heads[hh]] for r, (_, hh, _) in zip(rows, chains)]
        qc = [qs[r, heads[hh]] for r, (_, hh, _) in zip(rows, chains)]
        vf = [vs[r, heads[hh]].astype(F32) for r, (_, hh, _) in zip(rows, chains)]
        kf = _par(lambda x: x.astype(F32), kc)
        qf = _par(lambda x: x.astype(F32), qc)
        span = GDN_PREP_G * CHUNK
        bcg = []
        for idx in range(GDN_CHAINS):
            first = i * GDN_PREP_G if idx % 2 == 0 else nchunk - (i + 1) * GDN_PREP_G
            blk_rows = pl.ds(pl.multiple_of(first * CHUNK, span), span)
            bcg.append(_nn(cols_ref[0, blk_rows, :], esel_s[idx]))
        local = [(g if d == 0 else GDN_PREP_G - 1 - g) * CHUNK for g, _, d in chains]
        bc = [bcg[idx][r0:r0 + CHUNK, :] for idx, r0 in zip(chain_idx, local)]
        qkk = _par(lambda k_, q_: _nt(jnp.concatenate([k_, q_], axis=0), k_), kc, qc)
        yield
        beta_b, eg_b, ed_b, g_bc = (
            [x[:, i_ * LANES:(i_ + 1) * LANES] for x in bc] for i_ in range(N_BC + 1))
        g_col = [x[:, :CHUNK] for x in g_bc]
        g_row = [x.T[:CHUNK, :] for x in g_bc]
        el = [jnp.broadcast_to(eg[CHUNK - 1:, :] if d == 0 else eg[:1, :], (8, LANES))
              for d, eg in zip(chain_dir, eg_b)]
        decay = [jnp.exp(jnp.where(tri[d][0], gc_ - gr_, 0.0))
                 for d, gc_, gr_ in zip(chain_dir, g_col, g_row)]
        a = [jnp.where(tri[d][1], x[:CHUNK] * be[:, :CHUNK] * dc, 0.0)
             for d, x, be, dc in zip(chain_dir, qkk, beta_b, decay)]
        tmat = yield from _unit_triangular_inverses(a)
        rhs = [jnp.concatenate([bf(k_ * (be * eg)), bf(v_ * be)], axis=1)
               for k_, v_, be, eg in zip(kf, vf, beta_b, eg_b)]
        wu = _par(lambda t_, r_: bf(_nn(bf(t_), r_)), tmat, rhs)
        yield
        kdt = [bf((k_ * ed).T) for k_, ed in zip(kf, ed_b)]
        qkm = [bf(jnp.where(tri[d][0], x[CHUNK:] * dc, 0.0))
               for d, x, dc in zip(chain_dir, qkk, decay)]
        res = _par(lambda kt, qm, w_: _nn(jnp.concatenate([kt, qm], axis=0), w_),
                   kdt, qkm, wu)
        yield
        for idx, n, rs, q_, eg, el_ in zip(chain_idx, chunk, res, qf, eg_b, el):
            mq_s[idx, n, :HEAD_DIM] = bf(-rs[:HEAD_DIM, :HEAD_DIM])
            mq_s[idx, n, HEAD_DIM:] = bf(q_ * eg - rs[HEAD_DIM:, :HEAD_DIM])
            co_s[idx, n] = rs[:, HEAD_DIM:]
            el_s[idx, n] = el_

    def scan_step(s):
        for idx in range(GDN_CHAINS):
            n = s if idx % 2 == 0 else nchunk - 1 - s
            st = st_s[idx]
            x = co_s[idx, n] + _nn(mq_s[idx, n], st.astype(BF16))
            oacc[idx, pl.ds(pl.multiple_of(n * CHUNK, CHUNK), CHUNK), :] = x[HEAD_DIM:]
            el = jnp.broadcast_to(el_s[idx, n][:1, :], (HEAD_DIM, HEAD_DIM))
            st_s[idx] = st * el + x[:HEAD_DIM]

    def first_block(i, carry):
        _run_staged(prep_stages(i), [])
        return carry

    def block(i, carry):
        steps = [functools.partial(scan_step, (i - 1) * GDN_PREP_G + g)
                 for g in range(GDN_PREP_G)]
        _run_staged(prep_stages(i), steps)
        return carry

    def last_steps(s, carry):
        scan_step(s)
        return carry

    lax.fori_loop(0, 1, first_block, 0)
    lax.fori_loop(1, nblock, block, 0)
    lax.fori_loop(nchunk - GDN_PREP_G, nchunk, last_steps, 0)

    for hh, cs in enumerate(heads):
        o = oacc[2 * hh] + oacc[2 * hh + 1]
        o = o * lax.rsqrt(jnp.mean(o * o, axis=-1, keepdims=True) + RMS_EPS) * nw_ref[...]
        o_ref[0, :, cs] = (o * _silu(z_ref[0, :, cs].astype(F32))).astype(BF16)


def _gdn(proj3d, cols, conv_w, norm_w):
    b, l, _ = proj3d.shape
    nchunk = l // CHUNK
    assert DN_HEADS % GDN_HG == 0 and nchunk % GDN_PREP_G == 0
    groups = DN_HEADS // GDN_HG
    blk = lambda off: pl.BlockSpec((1, l, GDN_W), lambda i, h, off=off: (i, 0, off // GDN_W + h))
    wblk = lambda off: pl.BlockSpec((CONV_K, GDN_W), lambda i, h, off=off: (0, off // GDN_W + h))
    return pl.pallas_call(
        _gdn_kernel,
        grid=(b, groups),
        in_specs=[
            blk(OFF_DN), blk(OFF_DN + DN_WIDTH), blk(OFF_DN + 2 * DN_WIDTH), blk(OFF_Z),
            pl.BlockSpec((1, l, LANES), lambda i, h: (i, 0, 0)),
            wblk(0), wblk(DN_WIDTH), wblk(2 * DN_WIDTH),
            pl.BlockSpec((1, HEAD_DIM), lambda i, h: (0, 0)),
        ],
        out_specs=pl.BlockSpec((1, l, GDN_W), lambda i, h: (i, 0, h)),
        out_shape=jax.ShapeDtypeStruct((b, l, DN_WIDTH), BF16),
        scratch_shapes=[
            pltpu.VMEM((l + 2 * CONV_PAD, HEAD_DIM), F32),
            pltpu.VMEM((l, GDN_W), BF16),
            pltpu.VMEM((l, GDN_W), BF16),
            pltpu.VMEM((l, GDN_W), BF16),
            pltpu.VMEM((GDN_CHAINS, LANES, (N_BC + 1) * LANES), BF16),
            pltpu.VMEM((GDN_CHAINS, nchunk, MQ_ROWS, HEAD_DIM), BF16),
            pltpu.VMEM((GDN_CHAINS, nchunk, MQ_ROWS, HEAD_DIM), F32),
            pltpu.VMEM((GDN_CHAINS, nchunk, 8, HEAD_DIM), F32),
            pltpu.VMEM((GDN_CHAINS, HEAD_DIM, HEAD_DIM), F32),
            pltpu.VMEM((GDN_CHAINS, l, HEAD_DIM), F32),
        ],
        compiler_params=pltpu.CompilerParams(
            dimension_semantics=("arbitrary", "arbitrary"),
            vmem_limit_bytes=VMEM_LIMIT),
        name="gdn",
    )(proj3d, proj3d, proj3d, proj3d, cols, conv_w, conv_w, conv_w, norm_w)


def _out_proj_kernel(x_ref, yna_ref, ydn_ref, wa_ref, wb_ref, nw_ref, h_ref, hn_ref):
    mixed = _nn(yna_ref[...], wa_ref[...]) + _nn(ydn_ref[...], wb_ref[...])
    hres = x_ref[...] + mixed
    h_ref[...] = hres
    ms = jnp.mean(hres * hres, axis=-1, keepdims=True)
    hn_ref[...] = (hres * lax.rsqrt(ms + RMS_EPS) * nw_ref[...]).astype(BF16)


def _out_proj(x2d, yna, ydn, w_a, w_b, norm_w, tm=512):
    t = x2d.shape[0]
    row = lambda w: pl.BlockSpec((tm, w), lambda i: (i, 0))
    full = lambda a: pl.BlockSpec(a.shape, lambda i: (0, 0))
    return pl.pallas_call(
        _out_proj_kernel,
        grid=(t // tm,),
        in_specs=[row(D_MODEL), row(NA_WIDTH), row(DN_WIDTH), full(w_a), full(w_b),
                  full(norm_w)],
        out_specs=[row(D_MODEL), row(D_MODEL)],
        out_shape=[jax.ShapeDtypeStruct((t, D_MODEL), F32),
                   jax.ShapeDtypeStruct((t, D_MODEL), BF16)],
        compiler_params=pltpu.CompilerParams(
            dimension_semantics=("arbitrary",), vmem_limit_bytes=VMEM_LIMIT),
        name="out_proj",
    )(x2d, yna, ydn, w_a, w_b, norm_w)


def _ffn_kernel(hn_ref, h_ref, wg_ref, wu_ref, wd_ref, nw_ref, o_ref, acc_ref):
    f = pl.program_id(1)

    @pl.when(f == 0)
    def _():
        acc_ref[...] = h_ref[...]

    hn = hn_ref[...]
    g = _nn(hn, wg_ref[...])
    u = _nn(hn, wu_ref[...])
    acc_ref[...] += _nn((_silu(g) * u).astype(BF16), wd_ref[...])

    @pl.when(f == pl.num_programs(1) - 1)
    def _():
        y = acc_ref[...]
        ms = jnp.mean(y * y, axis=-1, keepdims=True)
        o_ref[...] = y * lax.rsqrt(ms + RMS_EPS) * nw_ref[...]


def _ffn(hn, hres, w_gate, w_up, w_down, norm_w, tm=512, tf=512):
    t = hn.shape[0]
    return pl.pallas_call(
        _ffn_kernel,
        grid=(t // tm, D_FF // tf),
        in_specs=[
            pl.BlockSpec((tm, D_MODEL), lambda i, f: (i, 0)),
            pl.BlockSpec((tm, D_MODEL), lambda i, f: (i, 0)),
            pl.BlockSpec((D_MODEL, tf), lambda i, f: (0, f)),
            pl.BlockSpec((D_MODEL, tf), lambda i, f: (0, f)),
            pl.BlockSpec((tf, D_MODEL), lambda i, f: (f, 0)),
            pl.BlockSpec((1, D_MODEL), lambda i, f: (0, 0)),
        ],
        out_specs=pl.BlockSpec((tm, D_MODEL), lambda i, f: (i, 0)),
        out_shape=jax.ShapeDtypeStruct((t, D_MODEL), F32),
        scratch_shapes=[pltpu.VMEM((tm, D_MODEL), F32)],
        compiler_params=pltpu.CompilerParams(
            dimension_semantics=("arbitrary", "arbitrary"),
            vmem_limit_bytes=VMEM_LIMIT),
        name="ffn",
    )(hn, hres, w_gate, w_up, w_down, norm_w)


def _lane_table(per_pair):
    return jnp.tile(per_pair.astype(F32), LANES // NPAIR).reshape(1, LANES)


def _trunk(x, p):
    b, l, _ = x.shape
    x2d = x.reshape(b * l, D_MODEL)
    proj, ba = _in_proj(x2d, p["norm_mix_w"], p["w_main"], p["w_ba"])
    proj3d = proj.reshape(b, l, MAIN_COLS)
    cols = _gates(ba.reshape(b, l, LANES), p["alog_lanes"], p["dtb_lanes"])
    y_na = _na(proj3d, p["rpb_flat"])
    y_dn = _gdn(proj3d, cols, p["conv_w"], p["dn_norm_w"])
    hres, hn = _out_proj(x2d, y_na.reshape(b * l, NA_WIDTH), y_dn.reshape(b * l, DN_WIDTH),
                         p["w_out_na"], p["w_out_dn"], p["norm_ffn_w"])
    y = _ffn(hn, hres, p["w_gate"], p["w_up"], p["w_down"], p["final_norm_w"])
    return y.reshape(b, l, D_MODEL)


def kernel(x_prompt, x_sample, norm_mix_w, w_in, na_rpb, dn_conv_w, dn_A_log, dn_dt_bias,
           dn_norm_w, w_out, norm_ffn_w, w_gate, w_up, w_down, final_norm_w):
    w_in0 = w_in[0]
    w_b = w_in0[:, OFF_B:OFF_A]
    w_a = w_in0[:, OFF_A:OFF_A + NPAIR]
    pad = jnp.zeros((D_MODEL, LANES - N_GATE_GROUPS * NPAIR), F32)
    p = {
        "norm_mix_w": norm_mix_w[0].reshape(1, D_MODEL),
        "w_main": w_in0.astype(BF16),
        "w_ba": jnp.concatenate([w_b] + [w_a] * (N_GATE_GROUPS - 1) + [pad],
                                axis=1).astype(BF16),
        "alog_lanes": _lane_table(dn_A_log[0].reshape(NPAIR)),
        "dtb_lanes": _lane_table(dn_dt_bias[0].reshape(NPAIR)),
        "rpb_flat": na_rpb[0].reshape(-1),
        "conv_w": dn_conv_w[0],
        "dn_norm_w": dn_norm_w[0].reshape(1, HEAD_DIM),
        "w_out_na": w_out[0, :NA_WIDTH].astype(BF16),
        "w_out_dn": w_out[0, NA_WIDTH:].astype(BF16),
        "norm_ffn_w": norm_ffn_w[0].reshape(1, D_MODEL),
        "w_gate": w_gate[0].astype(BF16),
        "w_up": w_up[0].astype(BF16),
        "w_down": w_down[0].astype(BF16),
        "final_norm_w": final_norm_w.reshape(1, D_MODEL),
    }
    return (_trunk(x_prompt, p), _trunk(x_sample, p))
```

```python
import functools

import jax
import jax.numpy as jnp
from jax import lax
from jax.experimental import pallas as pl
from jax.experimental.pallas import tpu as pltpu

D_MODEL = 2048
GRID_W = 64
HEAD_DIM = 128
NA_HEADS = 8
DN_HEADS = 8
NA_WIDTH = NA_HEADS * HEAD_DIM
DN_WIDTH = DN_HEADS * HEAD_DIM
NA_ROWS = 8
NA_COLS = 16
CONV_K = 5
CHUNK = 64
D_FF = 5632
RMS_EPS = 1e-6
L2_EPS = 1e-6

OFF_NA = 0
OFF_DN = OFF_NA + 3 * NA_WIDTH
OFF_Z = OFF_DN + 3 * DN_WIDTH
OFF_B = OFF_Z + DN_WIDTH
OFF_A = OFF_B + 2 * DN_HEADS
MAIN_COLS = OFF_B

LANES = 128
NPAIR = 2 * DN_HEADS
G_BETA, G_GHI, G_GLO, G_EG, G_ED = range(5)
N_GATE_GROUPS = 5
MASK_NEG = -1e30
VMEM_LIMIT = 56 * 1024 * 1024

F32 = jnp.float32
BF16 = jnp.bfloat16


def _silu(x):
    return x * (1.0 / (1.0 + jnp.exp(-x)))


def _nt(a, b):
    return lax.dot_general(a, b, (((1,), (1,)), ((), ())), preferred_element_type=F32)


def _nn(a, b):
    return jnp.dot(a, b, preferred_element_type=F32)


def _in_proj_kernel(x_ref, nw_ref, w_ref, wba_ref, proj_ref, ba_ref, xn_ref):
    @pl.when(pl.program_id(1) == 0)
    def _():
        x = x_ref[...]
        ms = jnp.mean(x * x, axis=-1, keepdims=True)
        xn = (x * lax.rsqrt(ms + RMS_EPS) * nw_ref[...]).astype(BF16)
        xn_ref[...] = xn
        ba_ref[...] = _nn(xn, wba_ref[...])

    proj_ref[...] = _nn(xn_ref[...], w_ref[...]).astype(BF16)


def _in_proj(x2d, norm_w, w_main, w_ba, tm=1024, tn=1792):
    t = x2d.shape[0]
    return pl.pallas_call(
        _in_proj_kernel,
        grid=(t // tm, MAIN_COLS // tn),
        in_specs=[
            pl.BlockSpec((tm, D_MODEL), lambda i, j: (i, 0)),
            pl.BlockSpec((1, D_MODEL), lambda i, j: (0, 0)),
            pl.BlockSpec((D_MODEL, tn), lambda i, j: (0, j)),
            pl.BlockSpec((D_MODEL, LANES), lambda i, j: (0, 0)),
        ],
        out_specs=[
            pl.BlockSpec((tm, tn), lambda i, j: (i, j)),
            pl.BlockSpec((tm, LANES), lambda i, j: (i, 0)),
        ],
        out_shape=[
            jax.ShapeDtypeStruct((t, MAIN_COLS), BF16),
            jax.ShapeDtypeStruct((t, LANES), F32),
        ],
        scratch_shapes=[pltpu.VMEM((tm, D_MODEL), BF16)],
        compiler_params=pltpu.CompilerParams(
            dimension_semantics=("arbitrary", "arbitrary"),
            vmem_limit_bytes=VMEM_LIMIT),
        name="in_proj",
    )(x2d, norm_w, w_main, w_ba)


def _gates_kernel(ba_ref, alog_ref, dtb_ref, out_ref):
    x = ba_ref[0]
    n = x.shape[0]
    lane = lax.broadcasted_iota(jnp.int32, x.shape, 1)
    pos = lax.broadcasted_iota(jnp.int32, x.shape, 0) % CHUNK
    grp = lane // NPAIR
    is_bwd = (lane % NPAIR) >= DN_HEADS

    beta = 1.0 / (1.0 + jnp.exp(-x))
    a = x + dtb_ref[...]
    softplus = jnp.maximum(a, 0.0) + jnp.log(1.0 + jnp.exp(-jnp.abs(a)))
    g = -jnp.exp(alog_ref[...]) * softplus

    pre = g
    suf = g
    s = 1
    while s < CHUNK:
        pre = pre + jnp.where(pos >= s, pltpu.roll(pre, s, axis=0), 0.0)
        suf = suf + jnp.where(pos < CHUNK - s, pltpu.roll(suf, n - s, axis=0), 0.0)
        s *= 2
    tot = pre + suf - g
    gc = jnp.where(is_bwd, suf, pre)
    ghi = gc.astype(BF16).astype(F32)

    out = jnp.where(grp == G_BETA, beta, 0.0)
    out = jnp.where(grp == G_GHI, ghi, out)
    out = jnp.where(grp == G_GLO, gc - ghi, out)
    out = jnp.where(grp == G_EG, jnp.exp(gc), out)
    out = jnp.where(grp == G_ED, jnp.exp(tot - gc), out)
    out_ref[0] = out.astype(BF16)


def _gates(ba3d, alog_lanes, dtb_lanes):
    b, l, _ = ba3d.shape
    return pl.pallas_call(
        _gates_kernel,
        grid=(b,),
        in_specs=[
            pl.BlockSpec((1, l, LANES), lambda i: (i, 0, 0)),
            pl.BlockSpec((1, LANES), lambda i: (0, 0)),
            pl.BlockSpec((1, LANES), lambda i: (0, 0)),
        ],
        out_specs=pl.BlockSpec((1, l, LANES), lambda i: (i, 0, 0)),
        out_shape=jax.ShapeDtypeStruct((b, l, LANES), BF16),
        compiler_params=pltpu.CompilerParams(
            dimension_semantics=("arbitrary",), vmem_limit_bytes=VMEM_LIMIT),
        name="gates",
    )(ba3d, alog_lanes, dtb_lanes)


N_DR = 2 * NA_ROWS - 1
N_DC = 2 * NA_COLS - 1
N_TAB = N_DR - 1
NA_ROW_G = 16


def _na_kernel(rpb_ref, q_ref, k_ref, v_ref, o_ref, tab_ref):
    h = pl.program_id(0)
    rows = q_ref.shape[1] // GRID_W
    band = NA_ROWS * GRID_W

    @pl.when(pl.program_id(1) == 0)
    def _():
        shape = (GRID_W, 2 * GRID_W)
        c = lax.broadcasted_iota(jnp.int32, shape, 0)
        lane = lax.broadcasted_iota(jnp.int32, shape, 1)
        kc = lane % GRID_W
        upper = lane >= GRID_W
        start = jnp.clip(c - NA_COLS // 2, 0, GRID_W - NA_COLS)
        valid = (kc >= start) & (kc < start + NA_COLS)
        delta = kc - c + (NA_COLS - 1)

        def build(d, carry):
            base = (h * N_DR + d) * N_DC
            val = jnp.full(shape, MASK_NEG, F32)
            for dc in range(N_DC):
                s0 = rpb_ref[base + dc]
                s1 = rpb_ref[base + N_DC + dc]
                val = jnp.where(delta == dc, jnp.where(upper, s1, s0), val)
            tab_ref[d] = jnp.where(valid, val, MASK_NEG)
            return carry

        lax.fori_loop(0, N_TAB, build, 0)

    scale = HEAD_DIM ** -0.5

    nparts = band // LANES

    def row_group(i, carry):
        rr = [i * NA_ROW_G + g for g in range(NA_ROW_G)]
        rs = [jnp.clip(r - NA_ROWS // 2, 0, rows - NA_ROWS) for r in rr]
        d0 = [s_ - r + (NA_ROWS - 1) for s_, r in zip(rs, rr)]
        qrow = [pl.ds(pl.multiple_of(r * GRID_W, GRID_W), GRID_W) for r in rr]
        brow = [pl.ds(pl.multiple_of(s_ * GRID_W, GRID_W), band) for s_ in rs]
        s = [_nt(q_ref[0, qr, :], k_ref[0, br, :]) for qr, br in zip(qrow, brow)]
        parts = [[x[:, j * LANES:(j + 1) * LANES] * scale + tab_ref[d + 2 * j]
                  for j in range(nparts)] for x, d in zip(s, d0)]
        m = [functools.reduce(jnp.maximum, ps_).max(axis=-1, keepdims=True) for ps_ in parts]
        e = [[jnp.exp(p_ - m_) for p_ in ps_] for ps_, m_ in zip(parts, m)]
        l = [functools.reduce(jnp.add, es_).sum(axis=-1, keepdims=True) for es_ in e]
        o = [_nn(jnp.concatenate(es_, axis=1).astype(BF16), v_ref[0, br, :])
             for es_, br in zip(e, brow)]
        for qr, o_, l_ in zip(qrow, o, l):
            o_ref[0, qr, :] = (o_ / l_).astype(BF16)
        return carry

    lax.fori_loop(0, rows // NA_ROW_G, row_group, 0)


def _na(proj3d, rpb_flat):
    b, l, _ = proj3d.shape
    blk = lambda off: pl.BlockSpec((1, l, HEAD_DIM), lambda h, i, off=off: (i, 0, off + h))
    return pl.pallas_call(
        _na_kernel,
        grid=(NA_HEADS, b),
        in_specs=[
            pl.BlockSpec(memory_space=pltpu.SMEM),
            blk(0), blk(NA_HEADS), blk(2 * NA_HEADS),
        ],
        out_specs=pl.BlockSpec((1, l, HEAD_DIM), lambda h, i: (i, 0, h)),
        out_shape=jax.ShapeDtypeStruct((b, l, NA_WIDTH), BF16),
        scratch_shapes=[pltpu.VMEM((N_TAB, GRID_W, 2 * GRID_W), F32)],
        compiler_params=pltpu.CompilerParams(
            dimension_semantics=("arbitrary", "arbitrary"),
            vmem_limit_bytes=VMEM_LIMIT),
        name="na",
    )(rpb_flat, proj3d, proj3d, proj3d)


SUB = 16


CONV_PAD = 8


CONV_TILE = 256


def _conv_silu(x_ref, cs, w, pad_ref, finish):
    n = x_ref.shape[1]
    half = CONV_K // 2

    def fill(t, carry):
        r0 = pl.multiple_of(t * CONV_TILE, CONV_TILE)
        pad_ref[pl.ds(CONV_PAD + r0, CONV_TILE), :] = (
            x_ref[0, pl.ds(r0, CONV_TILE), cs].astype(F32))
        return carry

    def tile(t, carry):
        r0 = pl.multiple_of(t * CONV_TILE, CONV_TILE)
        acc = None
        for i in range(CONV_K):
            term = pad_ref[pl.ds(r0 + (CONV_PAD + i - half), CONV_TILE), :] * w[i:i + 1, :]
            acc = term if acc is None else acc + term
        finish(r0, _silu(acc))
        return carry

    lax.fori_loop(0, n // CONV_TILE, fill, 0)
    lax.fori_loop(0, n // CONV_TILE, tile, 0, unroll=4)


def _l2norm(x):
    return x * lax.rsqrt(jnp.sum(x * x, axis=-1, keepdims=True) + L2_EPS)


def _par(f, *lists):
    return [f(*xs) for xs in zip(*lists)]


def _unit_triangular_inverses(mats):
    n = mats[0].shape[0]
    i = lax.broadcasted_iota(jnp.int32, (n, n), 0)
    j = lax.broadcasted_iota(jnp.int32, (n, n), 1)
    same = (i // SUB) == (j // SUB)
    eye = jnp.where(i == j, 1.0, 0.0)
    bf = lambda x: x.astype(BF16)
    d = _par(lambda a: jnp.where(same, a, 0.0), mats)
    rest = _par(lambda a, dd: bf(a - dd), mats, d)
    dp = _par(bf, d)
    p = _par(lambda dd: eye - dd, d)
    step = 1
    while 2 * step < SUB:
        dp = _par(lambda x: bf(_nn(x, x)), dp)
        yield
        p = _par(lambda pp, x: pp + _nn(bf(pp), x), p, dp)
        yield
        step *= 2
    m = _par(lambda pp, rr: _nn(bf(pp), rr), p, rest)
    yield
    m2 = _par(lambda mm: bf(_nn(bf(mm), bf(mm))), m)
    yield
    r = _par(lambda mm, sq: _nn(bf(eye - mm), sq) - mm, m, m2)
    yield
    return _par(lambda pp, rr: pp + _nn(bf(rr), bf(pp)), p, r)


def _run_staged(stages, extras):
    extras = list(extras)
    for k, _ in enumerate(stages):
        if extras and k % 2 == 0:
            extras.pop(0)()
    for f in extras:
        f()


GDN_HG = 2
GDN_PREP_G = 8
GDN_CHAINS = 2 * GDN_HG
GDN_W = GDN_HG * HEAD_DIM
MQ_ROWS = HEAD_DIM + CHUNK
BCAST_GROUPS = (G_BETA, G_EG, G_ED)
N_BC = len(BCAST_GROUPS)


def _gdn_kernel(q_ref, k_ref, v_ref, z_ref, cols_ref, wq_ref, wk_ref, wv_ref, nw_ref,
                o_ref, pad_s, qs, ks, vs, esel_s, mq_s, co_s, el_s, st_s, oacc):
    hg = pl.program_id(1)
    l = q_ref.shape[1]
    nchunk = l // CHUNK
    heads = [slice(hh * HEAD_DIM, (hh + 1) * HEAD_DIM) for hh in range(GDN_HG)]

    zero_pad = jnp.zeros((CONV_PAD, HEAD_DIM), F32)
    pad_s[:CONV_PAD, :] = zero_pad
    pad_s[CONV_PAD + l:, :] = zero_pad
    for cs in heads:
        def put_q(r0, y, cs=cs):
            qs[pl.ds(r0, CONV_TILE), cs] = (_l2norm(y) * (HEAD_DIM ** -0.5)).astype(BF16)

        def put_k(r0, y, cs=cs):
            ks[pl.ds(r0, CONV_TILE), cs] = _l2norm(y).astype(BF16)

        def put_v(r0, y, cs=cs):
            vs[pl.ds(r0, CONV_TILE), cs] = y.astype(BF16)

        _conv_silu(q_ref, cs, wq_ref[:, cs], pad_s, put_q)
        _conv_silu(k_ref, cs, wk_ref[:, cs], pad_s, put_k)
        _conv_silu(v_ref, cs, wv_ref[:, cs], pad_s, put_v)

    bc_shape = (LANES, N_BC * LANES)
    bc_row = lax.broadcasted_iota(jnp.int32, bc_shape, 0)
    bc_blk = lax.broadcasted_iota(jnp.int32, bc_shape, 1) // LANES
    bc_grp = jnp.zeros(bc_shape, jnp.int32)
    for i, g in enumerate(BCAST_GROUPS):
        bc_grp = jnp.where(bc_blk == i, g, bc_grp)
    sel_r = lax.broadcasted_iota(jnp.int32, (LANES, LANES), 0)
    for idx in range(GDN_CHAINS):
        pair = (idx % 2) * DN_HEADS + hg * GDN_HG + idx // 2
        hi, lo = G_GHI * NPAIR + pair, G_GLO * NPAIR + pair
        esel_s[idx, :, :N_BC * LANES] = jnp.where(
            bc_row == bc_grp * NPAIR + pair, 1.0, 0.0).astype(BF16)
        esel_s[idx, :, N_BC * LANES:] = jnp.where(
            (sel_r == hi) | (sel_r == lo), 1.0, 0.0).astype(BF16)
        st_s[idx] = jnp.zeros((HEAD_DIM, HEAD_DIM), F32)

    ri = lax.broadcasted_iota(jnp.int32, (CHUNK, CHUNK), 0)
    ci = lax.broadcasted_iota(jnp.int32, (CHUNK, CHUNK), 1)

    bf = lambda x: x.astype(BF16)
    tri = ((ri >= ci, ri > ci), (ri <= ci, ri < ci))

    nblock = nchunk // GDN_PREP_G
    chains = [(g, hh, d) for g in range(GDN_PREP_G) for hh in range(GDN_HG) for d in range(2)]
    chain_idx = [2 * hh + d for _, hh, d in chains]
    chain_dir = [d for _, _, d in chains]

    def prep_stages(i):
        step = [i * GDN_PREP_G + g for g, _, _ in chains]
        chunk = [s if d == 0 else nchunk - 1 - s for s, d in zip(step, chain_dir)]
        rows = [pl.ds(pl.multiple_of(n * CHUNK, CHUNK), CHUNK) for n in chunk]
        kc = [ks[r, heads[hh]] for r, (_, hh, _) in zip(rows, chains)]
        qc = [qs[r, heads[hh]] for r, (_, hh, _) in zip(rows, chains)]
        vf = [vs[r, heads[hh]].astype(F32) for r, (_, hh, _) in zip(rows, chains)]
        kf = _par(lambda x: x.astype(F32), kc)
        qf = _par(lambda x: x.astype(F32), qc)
        span = GDN_PREP_G * CHUNK
        bcg = []
        for idx in range(GDN_CHAINS):
            first = i * GDN_PREP_G if idx % 2 == 0 else nchunk - (i + 1) * GDN_PREP_G
            blk_rows = pl.ds(pl.multiple_of(first * CHUNK, span), span)
            bcg.append(_nn(cols_ref[0, blk_rows, :], esel_s[idx]))
        local = [(g if d == 0 else GDN_PREP_G - 1 - g) * CHUNK for g, _, d in chains]
        bc = [bcg[idx][r0:r0 + CHUNK, :] for idx, r0 in zip(chain_idx, local)]
        qkk = _par(lambda k_, q_: _nt(jnp.concatenate([k_, q_], axis=0), k_), kc, qc)
        yield
        beta_b, eg_b, ed_b, g_bc = (
            [x[:, i_ * LANES:(i_ + 1) * LANES] for x in bc] for i_ in range(N_BC + 1))
        g_col = [x[:, :CHUNK] for x in g_bc]
        g_row = [x.T[:CHUNK, :] for x in g_bc]
        el = [jnp.broadcast_to(eg[CHUNK - 1:, :] if d == 0 else eg[:1, :], (8, LANES))
              for d, eg in zip(chain_dir, eg_b)]
        decay = [jnp.exp(jnp.where(tri[d][0], gc_ - gr_, 0.0))
                 for d, gc_, gr_ in zip(chain_dir, g_col, g_row)]
        a = [jnp.where(tri[d][1], x[:CHUNK] * be[:, :CHUNK] * dc, 0.0)
             for d, x, be, dc in zip(chain_dir, qkk, beta_b, decay)]
        tmat = yield from _unit_triangular_inverses(a)
        rhs = [jnp.concatenate([bf(k_ * (be * eg)), bf(v_ * be)], axis=1)
               for k_, v_, be, eg in zip(kf, vf, beta_b, eg_b)]
        wu = _par(lambda t_, r_: bf(_nn(bf(t_), r_)), tmat, rhs)
        yield
        kdt = [bf((k_ * ed).T) for k_, ed in zip(kf, ed_b)]
        qkm = [bf(jnp.where(tri[d][0], x[CHUNK:] * dc, 0.0))
               for d, x, dc in zip(chain_dir, qkk, decay)]
        res = _par(lambda kt, qm, w_: _nn(jnp.concatenate([kt, qm], axis=0), w_),
                   kdt, qkm, wu)
        yield
        for idx, n, rs, q_, eg, el_ in zip(chain_idx, chunk, res, qf, eg_b, el):
            mq_s[idx, n, :HEAD_DIM] = bf(-rs[:HEAD_DIM, :HEAD_DIM])
            mq_s[idx, n, HEAD_DIM:] = bf(q_ * eg - rs[HEAD_DIM:, :HEAD_DIM])
            co_s[idx, n] = rs[:, HEAD_DIM:]
            el_s[idx, n] = el_

    def scan_step(s):
        for idx in range(GDN_CHAINS):
            n = s if idx % 2 == 0 else nchunk - 1 - s
            st = st_s[idx]
            x = co_s[idx, n] + _nn(mq_s[idx, n], st.astype(BF16))
            oacc[idx, pl.ds(pl.multiple_of(n * CHUNK, CHUNK), CHUNK), :] = x[HEAD_DIM:]
            el = jnp.broadcast_to(el_s[idx, n][:1, :], (HEAD_DIM, HEAD_DIM))
            st_s[idx] = st * el + x[:HEAD_DIM]

    def first_block(i, carry):
        _run_staged(prep_stages(i), [])
        return carry

    def block(i, carry):
        steps = [functools.partial(scan_step, (i - 1) * GDN_PREP_G + g)
                 for g in range(GDN_PREP_G)]
        _run_staged(prep_stages(i), steps)
        return carry

    def last_steps(s, carry):
        scan_step(s)
        return carry

    lax.fori_loop(0, 1, first_block, 0)
    lax.fori_loop(1, nblock, block, 0)
    lax.fori_loop(nchunk - GDN_PREP_G, nchunk, last_steps, 0)

    for hh, cs in enumerate(heads):
        o = oacc[2 * hh] + oacc[2 * hh + 1]
        o = o * lax.rsqrt(jnp.mean(o * o, axis=-1, keepdims=True) + RMS_EPS) * nw_ref[...]
        o_ref[0, :, cs] = (o * _silu(z_ref[0, :, cs].astype(F32))).astype(BF16)


def _gdn(proj3d, cols, conv_w, norm_w):
    b, l, _ = proj3d.shape
    nchunk = l // CHUNK
    assert DN_HEADS % GDN_HG == 0 and nchunk % GDN_PREP_G == 0
    groups = DN_HEADS // GDN_HG
    blk = lambda off: pl.BlockSpec((1, l, GDN_W), lambda i, h, off=off: (i, 0, off // GDN_W + h))
    wblk = lambda off: pl.BlockSpec((CONV_K, GDN_W), lambda i, h, off=off: (0, off // GDN_W + h))
    return pl.pallas_call(
        _gdn_kernel,
        grid=(b, groups),
        in_specs=[
            blk(OFF_DN), blk(OFF_DN + DN_WIDTH), blk(OFF_DN + 2 * DN_WIDTH), blk(OFF_Z),
            pl.BlockSpec((1, l, LANES), lambda i, h: (i, 0, 0)),
            wblk(0), wblk(DN_WIDTH), wblk(2 * DN_WIDTH),
            pl.BlockSpec((1, HEAD_DIM), lambda i, h: (0, 0)),
        ],
        out_specs=pl.BlockSpec((1, l, GDN_W), lambda i, h: (i, 0, h)),
        out_shape=jax.ShapeDtypeStruct((b, l, DN_WIDTH), BF16),
        scratch_shapes=[
            pltpu.VMEM((l + 2 * CONV_PAD, HEAD_DIM), F32),
            pltpu.VMEM((l, GDN_W), BF16),
            pltpu.VMEM((l, GDN_W), BF16),
            pltpu.VMEM((l, GDN_W), BF16),
            pltpu.VMEM((GDN_CHAINS, LANES, (N_BC + 1) * LANES), BF16),
            pltpu.VMEM((GDN_CHAINS, nchunk, MQ_ROWS, HEAD_DIM), BF16),
            pltpu.VMEM((GDN_CHAINS, nchunk, MQ_ROWS, HEAD_DIM), F32),
            pltpu.VMEM((GDN_CHAINS, nchunk, 8, HEAD_DIM), F32),
            pltpu.VMEM((GDN_CHAINS, HEAD_DIM, HEAD_DIM), F32),
            pltpu.VMEM((GDN_CHAINS, l, HEAD_DIM), F32),
        ],
        compiler_params=pltpu.CompilerParams(
            dimension_semantics=("arbitrary", "arbitrary"),
            vmem_limit_bytes=VMEM_LIMIT),
        name="gdn",
    )(proj3d, proj3d, proj3d, proj3d, cols, conv_w, conv_w, conv_w, norm_w)


def _out_proj_kernel(x_ref, yna_ref, ydn_ref, wa_ref, wb_ref, nw_ref, h_ref, hn_ref):
    mixed = _nn(yna_ref[...], wa_ref[...]) + _nn(ydn_ref[...], wb_ref[...])
    hres = x_ref[...] + mixed
    h_ref[...] = hres
    ms = jnp.mean(hres * hres, axis=-1, keepdims=True)
    hn_ref[...] = (hres * lax.rsqrt(ms + RMS_EPS) * nw_ref[...]).astype(BF16)


def _out_proj(x2d, yna, ydn, w_a, w_b, norm_w, tm=512):
    t = x2d.shape[0]
    row = lambda w: pl.BlockSpec((tm, w), lambda i: (i, 0))
    full = lambda a: pl.BlockSpec(a.shape, lambda i: (0, 0))
    return pl.pallas_call(
        _out_proj_kernel,
        grid=(t // tm,),
        in_specs=[row(D_MODEL), row(NA_WIDTH), row(DN_WIDTH), full(w_a), full(w_b),
                  full(norm_w)],
        out_specs=[row(D_MODEL), row(D_MODEL)],
        out_shape=[jax.ShapeDtypeStruct((t, D_MODEL), F32),
                   jax.ShapeDtypeStruct((t, D_MODEL), BF16)],
        compiler_params=pltpu.CompilerParams(
            dimension_semantics=("arbitrary",), vmem_limit_bytes=VMEM_LIMIT),
        name="out_proj",
    )(x2d, yna, ydn, w_a, w_b, norm_w)


def _ffn_kernel(hn_ref, h_ref, wg_ref, wu_ref, wd_ref, nw_ref, o_ref, acc_ref):
    f = pl.program_id(1)

    @pl.when(f == 0)
    def _():
        acc_ref[...] = h_ref[...]

    hn = hn_ref[...]
    g = _nn(hn, wg_ref[...])
    u = _nn(hn, wu_ref[...])
    acc_ref[...] += _nn((_silu(g) * u).astype(BF16), wd_ref[...])

    @pl.when(f == pl.num_programs(1) - 1)
    def _():
        y = acc_ref[...]
        ms = jnp.mean(y * y, axis=-1, keepdims=True)
        o_ref[...] = y * lax.rsqrt(ms + RMS_EPS) * nw_ref[...]


def _ffn(hn, hres, w_gate, w_up, w_down, norm_w, tm=512, tf=512):
    t = hn.shape[0]
    return pl.pallas_call(
        _ffn_kernel,
        grid=(t // tm, D_FF // tf),
        in_specs=[
            pl.BlockSpec((tm, D_MODEL), lambda i, f: (i, 0)),
            pl.BlockSpec((tm, D_MODEL), lambda i, f: (i, 0)),
            pl.BlockSpec((D_MODEL, tf), lambda i, f: (0, f)),
            pl.BlockSpec((D_MODEL, tf), lambda i, f: (0, f)),
            pl.BlockSpec((tf, D_MODEL), lambda i, f: (f, 0)),
            pl.BlockSpec((1, D_MODEL), lambda i, f: (0, 0)),
        ],
        out_specs=pl.BlockSpec((tm, D_MODEL), lambda i, f: (i, 0)),
        out_shape=jax.ShapeDtypeStruct((t, D_MODEL), F32),
        scratch_shapes=[pltpu.VMEM((tm, D_MODEL), F32)],
        compiler_params=pltpu.CompilerParams(
            dimension_semantics=("arbitrary", "arbitrary"),
            vmem_limit_bytes=VMEM_LIMIT),
        name="ffn",
    )(hn, hres, w_gate, w_up, w_down, norm_w)


def _lane_table(per_pair):
    return jnp.tile(per_pair.astype(F32), LANES // NPAIR).reshape(1, LANES)


def _trunk(x, p):
    b, l, _ = x.shape
    x2d = x.reshape(b * l, D_MODEL)
    proj, ba = _in_proj(x2d, p["norm_mix_w"], p["w_main"], p["w_ba"])
    proj3d = proj.reshape(b, l, MAIN_COLS)
    cols = _gates(ba.reshape(b, l, LANES), p["alog_lanes"], p["dtb_lanes"])
    y_na = _na(proj3d, p["rpb_flat"])
    y_dn = _gdn(proj3d, cols, p["conv_w"], p["dn_norm_w"])
    hres, hn = _out_proj(x2d, y_na.reshape(b * l, NA_WIDTH), y_dn.reshape(b * l, DN_WIDTH),
                         p["w_out_na"], p["w_out_dn"], p["norm_ffn_w"])
    y = _ffn(hn, hres, p["w_gate"], p["w_up"], p["w_down"], p["final_norm_w"])
    return y.reshape(b, l, D_MODEL)


def kernel(x_prompt, x_sample, norm_mix_w, w_in, na_rpb, dn_conv_w, dn_A_log, dn_dt_bias,
           dn_norm_w, w_out, norm_ffn_w, w_gate, w_up, w_down, final_norm_w):
    w_in0 = w_in[0]
    w_b = w_in0[:, OFF_B:OFF_A]
    w_a = w_in0[:, OFF_A:OFF_A + NPAIR]
    pad = jnp.zeros((D_MODEL, LANES - N_GATE_GROUPS * NPAIR), F32)
    p = {
        "norm_mix_w": norm_mix_w[0].reshape(1, D_MODEL),
        "w_main": w_in0.astype(BF16),
        "w_ba": jnp.concatenate([w_b] + [w_a] * (N_GATE_GROUPS - 1) + [pad],
                                axis=1).astype(BF16),
        "alog_lanes": _lane_table(dn_A_log[0].reshape(NPAIR)),
        "dtb_lanes": _lane_table(dn_dt_bias[0].reshape(NPAIR)),
        "rpb_flat": na_rpb[0].reshape(-1),
        "conv_w": dn_conv_w[0],
        "dn_norm_w": dn_norm_w[0].reshape(1, HEAD_DIM),
        "w_out_na": w_out[0, :NA_WIDTH].astype(BF16),
        "w_out_dn": w_out[0, NA_WIDTH:].astype(BF16),
        "norm_ffn_w": norm_ffn_w[0].reshape(1, D_MODEL),
        "w_gate": w_gate[0].astype(BF16),
        "w_up": w_up[0].astype(BF16),
        "w_down": w_down[0].astype(BF16),
        "final_norm_w": final_norm_w.reshape(1, D_MODEL),
    }
    return (_trunk(x_prompt, p), _trunk(x_sample, p))
```

```python
import functools

import jax
import jax.numpy as jnp
from jax import lax
from jax.experimental import pallas as pl
from jax.experimental.pallas import tpu as pltpu

D_MODEL = 2048
GRID_W = 64
HEAD_DIM = 128
NA_HEADS = 8
DN_HEADS = 8
NA_WIDTH = NA_HEADS * HEAD_DIM
DN_WIDTH = DN_HEADS * HEAD_DIM
NA_ROWS = 8
NA_COLS = 16
CONV_K = 5
CHUNK = 64
D_FF = 5632
RMS_EPS = 1e-6
L2_EPS = 1e-6

OFF_NA = 0
OFF_DN = OFF_NA + 3 * NA_WIDTH
OFF_Z = OFF_DN + 3 * DN_WIDTH
OFF_B = OFF_Z + DN_WIDTH
OFF_A = OFF_B + 2 * DN_HEADS
MAIN_COLS = OFF_B

LANES = 128
SUBLANES = 8
NPAIR = 2 * DN_HEADS
G_BETA, G_GHI, G_GLO = range(3)
N_GATE_GROUPS = 3
MASK_NEG = -1e30
VMEM_LIMIT = 60 * 1024 * 1024

F32 = jnp.float32
BF16 = jnp.bfloat16


def _silu(x):
    return x * (1.0 / (1.0 + jnp.exp(-x)))


def _nt(a, b):
    return lax.dot_general(a, b, (((1,), (1,)), ((), ())), preferred_element_type=F32)


def _nn(a, b):
    return jnp.dot(a, b, preferred_element_type=F32)


def _in_proj_kernel(x_ref, nw_ref, w_ref, wba_ref, proj_ref, ba_ref, xn_ref):
    @pl.when(pl.program_id(1) == 0)
    def _():
        x = x_ref[...]
        ms = jnp.mean(x * x, axis=-1, keepdims=True)
        xn = (x * lax.rsqrt(ms + RMS_EPS) * nw_ref[...]).astype(BF16)
        xn_ref[...] = xn
        ba_ref[...] = _nn(xn, wba_ref[...])

    proj_ref[...] = _nn(xn_ref[...], w_ref[...]).astype(BF16)


def _in_proj(x2d, norm_w, w_main, w_ba, tm=1024, tn=1792):
    t = x2d.shape[0]
    return pl.pallas_call(
        _in_proj_kernel,
        grid=(t // tm, MAIN_COLS // tn),
        in_specs=[
            pl.BlockSpec((tm, D_MODEL), lambda i, j: (i, 0)),
            pl.BlockSpec((1, D_MODEL), lambda i, j: (0, 0)),
            pl.BlockSpec((D_MODEL, tn), lambda i, j: (0, j)),
            pl.BlockSpec((D_MODEL, LANES), lambda i, j: (0, 0)),
        ],
        out_specs=[
            pl.BlockSpec((tm, tn), lambda i, j: (i, j)),
            pl.BlockSpec((tm, LANES), lambda i, j: (i, 0)),
        ],
        out_shape=[
            jax.ShapeDtypeStruct((t, MAIN_COLS), BF16),
            jax.ShapeDtypeStruct((t, LANES), F32),
        ],
        scratch_shapes=[pltpu.VMEM((tm, D_MODEL), BF16)],
        compiler_params=pltpu.CompilerParams(
            dimension_semantics=("arbitrary", "arbitrary"),
            vmem_limit_bytes=VMEM_LIMIT),
        name="in_proj",
    )(x2d, norm_w, w_main, w_ba)


def _gates_kernel(ba_ref, alog_ref, dtb_ref, out_ref):
    x = ba_ref[0]
    n = x.shape[0]
    lane = lax.broadcasted_iota(jnp.int32, x.shape, 1)
    pos = lax.broadcasted_iota(jnp.int32, x.shape, 0) % CHUNK
    grp = lane // NPAIR
    is_bwd = (lane % NPAIR) >= DN_HEADS

    beta = 1.0 / (1.0 + jnp.exp(-x))
    a = x + dtb_ref[...]
    softplus = jnp.maximum(a, 0.0) + jnp.log(1.0 + jnp.exp(-jnp.abs(a)))
    g = -jnp.exp(alog_ref[...]) * softplus

    pre = g
    suf = g
    s = 1
    while s < CHUNK:
        pre = pre + jnp.where(pos >= s, pltpu.roll(pre, s, axis=0), 0.0)
        suf = suf + jnp.where(pos < CHUNK - s, pltpu.roll(suf, n - s, axis=0), 0.0)
        s *= 2
    gc = jnp.where(is_bwd, suf, pre)
    ghi = gc.astype(BF16).astype(F32)

    out = jnp.where(grp == G_BETA, beta, 0.0)
    out = jnp.where(grp == G_GHI, ghi, out)
    out = jnp.where(grp == G_GLO, gc - ghi, out)
    out_ref[0] = out.astype(BF16)


def _gates(ba3d, alog_lanes, dtb_lanes):
    b, l, _ = ba3d.shape
    return pl.pallas_call(
        _gates_kernel,
        grid=(b,),
        in_specs=[
            pl.BlockSpec((1, l, LANES), lambda i: (i, 0, 0)),
            pl.BlockSpec((1, LANES), lambda i: (0, 0)),
            pl.BlockSpec((1, LANES), lambda i: (0, 0)),
        ],
        out_specs=pl.BlockSpec((1, l, LANES), lambda i: (i, 0, 0)),
        out_shape=jax.ShapeDtypeStruct((b, l, LANES), BF16),
        compiler_params=pltpu.CompilerParams(
            dimension_semantics=("arbitrary",), vmem_limit_bytes=VMEM_LIMIT),
        name="gates",
    )(ba3d, alog_lanes, dtb_lanes)


N_DR = 2 * NA_ROWS - 1
N_DC = 2 * NA_COLS - 1
N_TAB = N_DR - 1
NA_ROW_G = 32


def _na_kernel(rpb_ref, q_ref, k_ref, v_ref, o_ref, tab_ref):
    h = pl.program_id(0)
    rows = q_ref.shape[1] // GRID_W
    band = NA_ROWS * GRID_W

    @pl.when(pl.program_id(1) == 0)
    def _():
        shape = (GRID_W, 2 * GRID_W)
        c = lax.broadcasted_iota(jnp.int32, shape, 0)
        lane = lax.broadcasted_iota(jnp.int32, shape, 1)
        kc = lane % GRID_W
        upper = lane >= GRID_W
        start = jnp.clip(c - NA_COLS // 2, 0, GRID_W - NA_COLS)
        valid = (kc >= start) & (kc < start + NA_COLS)
        delta = kc - c + (NA_COLS - 1)

        def build(d, carry):
            base = (h * N_DR + d) * N_DC
            val = jnp.full(shape, MASK_NEG, F32)
            for dc in range(N_DC):
                s0 = rpb_ref[base + dc]
                s1 = rpb_ref[base + N_DC + dc]
                val = jnp.where(delta == dc, jnp.where(upper, s1, s0), val)
            tab_ref[d] = jnp.where(valid, val, MASK_NEG)
            return carry

        lax.fori_loop(0, N_TAB, build, 0)

    scale = HEAD_DIM ** -0.5

    nparts = band // LANES

    def row_group(i, carry):
        rr = [i * NA_ROW_G + g for g in range(NA_ROW_G)]
        rs = [jnp.clip(r - NA_ROWS // 2, 0, rows - NA_ROWS) for r in rr]
        d0 = [s_ - r + (NA_ROWS - 1) for s_, r in zip(rs, rr)]
        qrow = [pl.ds(pl.multiple_of(r * GRID_W, GRID_W), GRID_W) for r in rr]
        brow = [pl.ds(pl.multiple_of(s_ * GRID_W, GRID_W), band) for s_ in rs]
        s = [_nt(q_ref[0, qr, :], k_ref[0, br, :]) for qr, br in zip(qrow, brow)]
        parts = [[x[:, j * LANES:(j + 1) * LANES] * scale + tab_ref[d + 2 * j]
                  for j in range(nparts)] for x, d in zip(s, d0)]
        m = [functools.reduce(jnp.maximum, ps_).max(axis=-1, keepdims=True) for ps_ in parts]
        e = [[jnp.exp(p_ - m_) for p_ in ps_] for ps_, m_ in zip(parts, m)]
        l = [functools.reduce(jnp.add, es_).sum(axis=-1, keepdims=True) for es_ in e]
        o = [_nn(jnp.concatenate(es_, axis=1).astype(BF16), v_ref[0, br, :])
             for es_, br in zip(e, brow)]
        for qr, o_, l_ in zip(qrow, o, l):
            o_ref[0, qr, :] = (o_ / l_).astype(BF16)
        return carry

    lax.fori_loop(0, rows // NA_ROW_G, row_group, 0)


def _na(proj3d, rpb_flat):
    b, l, _ = proj3d.shape
    blk = lambda off: pl.BlockSpec((1, l, HEAD_DIM), lambda h, i, off=off: (i, 0, off + h))
    return pl.pallas_call(
        _na_kernel,
        grid=(NA_HEADS, b),
        in_specs=[
            pl.BlockSpec(memory_space=pltpu.SMEM),
            blk(0), blk(NA_HEADS), blk(2 * NA_HEADS),
        ],
        out_specs=pl.BlockSpec((1, l, HEAD_DIM), lambda h, i: (i, 0, h)),
        out_shape=jax.ShapeDtypeStruct((b, l, NA_WIDTH), BF16),
        scratch_shapes=[pltpu.VMEM((N_TAB, GRID_W, 2 * GRID_W), F32)],
        compiler_params=pltpu.CompilerParams(
            dimension_semantics=("arbitrary", "arbitrary"),
            vmem_limit_bytes=VMEM_LIMIT),
        name="na",
    )(rpb_flat, proj3d, proj3d, proj3d)


SUB = 16


CONV_PAD = SUBLANES


CONV_TILE = 256


def _conv_silu(x_ref, cs, w, pad_ref, finish):
    n = x_ref.shape[1]
    half = CONV_K // 2

    def fill(t, carry):
        r0 = pl.multiple_of(t * CONV_TILE, CONV_TILE)
        pad_ref[pl.ds(CONV_PAD + r0, CONV_TILE), :] = (
            x_ref[0, pl.ds(r0, CONV_TILE), cs].astype(F32))
        return carry

    def tile(t, carry):
        r0 = pl.multiple_of(t * CONV_TILE, CONV_TILE)
        acc = None
        for i in range(CONV_K):
            term = pad_ref[pl.ds(r0 + (CONV_PAD + i - half), CONV_TILE), :] * w[i:i + 1, :]
            acc = term if acc is None else acc + term
        finish(r0, _silu(acc))
        return carry

    lax.fori_loop(0, n // CONV_TILE, fill, 0)
    lax.fori_loop(0, n // CONV_TILE, tile, 0, unroll=4)


def _l2norm(x):
    return x * lax.rsqrt(jnp.sum(x * x, axis=-1, keepdims=True) + L2_EPS)


def _par(f, *lists):
    return [f(*xs) for xs in zip(*lists)]


def _unit_triangular_inverses(mats):
    n = mats[0].shape[0]
    i = lax.broadcasted_iota(jnp.int32, (n, n), 0)
    j = lax.broadcasted_iota(jnp.int32, (n, n), 1)
    same = (i // SUB) == (j // SUB)
    eye = jnp.where(i == j, 1.0, 0.0)
    bf = lambda x: x.astype(BF16)
    d = _par(lambda a: jnp.where(same, a, 0.0), mats)
    rest = _par(lambda a, dd: bf(a - dd), mats, d)
    dp = _par(bf, d)
    p = _par(lambda dd: eye - dd, d)
    step = 1
    while 2 * step < SUB:
        dp = _par(lambda x: bf(_nn(x, x)), dp)
        yield
        p = _par(lambda pp, x: pp + _nn(bf(pp), x), p, dp)
        yield
        step *= 2
    m = _par(lambda pp, rr: _nn(bf(pp), rr), p, rest)
    yield
    m2 = _par(lambda mm: bf(_nn(bf(mm), bf(mm))), m)
    yield
    r = _par(lambda mm, sq: _nn(bf(eye - mm), sq) - mm, m, m2)
    yield
    return _par(lambda pp, rr: pp + _nn(bf(rr), bf(pp)), p, r)


def _run_staged(stages, extras):
    extras = list(extras)
    for k, _ in enumerate(stages):
        if extras and k % 2 == 0:
            extras.pop(0)()
    for f in extras:
        f()


GDN_HG = 2
GDN_PREP_G = 8
GDN_CHAINS = 2 * GDN_HG
GDN_W = GDN_HG * HEAD_DIM
MQ_ROWS = HEAD_DIM + CHUNK
BCAST_GROUPS = (G_BETA,)
N_BC = len(BCAST_GROUPS)


def _gdn_kernel(q_ref, k_ref, v_ref, z_ref, cols_ref, wq_ref, wk_ref, wv_ref, nw_ref,
                o_ref, pad_s, qs, ks, vs, esel_s, mq_s, co_s, el_s, st_s, oacc):
    hg = pl.program_id(1)
    l = q_ref.shape[1]
    nchunk = l // CHUNK
    heads = [slice(hh * HEAD_DIM, (hh + 1) * HEAD_DIM) for hh in range(GDN_HG)]

    zero_pad = jnp.zeros((CONV_PAD, HEAD_DIM), F32)
    pad_s[:CONV_PAD, :] = zero_pad
    pad_s[CONV_PAD + l:, :] = zero_pad
    for cs in heads:
        def put_q(r0, y, cs=cs):
            qs[pl.ds(r0, CONV_TILE), cs] = (_l2norm(y) * (HEAD_DIM ** -0.5)).astype(BF16)

        def put_k(r0, y, cs=cs):
            ks[pl.ds(r0, CONV_TILE), cs] = _l2norm(y).astype(BF16)

        def put_v(r0, y, cs=cs):
            vs[pl.ds(r0, CONV_TILE), cs] = y.astype(BF16)

        _conv_silu(q_ref, cs, wq_ref[:, cs], pad_s, put_q)
        _conv_silu(k_ref, cs, wk_ref[:, cs], pad_s, put_k)
        _conv_silu(v_ref, cs, wv_ref[:, cs], pad_s, put_v)

    bc_shape = (LANES, N_BC * LANES)
    bc_row = lax.broadcasted_iota(jnp.int32, bc_shape, 0)
    bc_blk = lax.broadcasted_iota(jnp.int32, bc_shape, 1) // LANES
    bc_grp = jnp.zeros(bc_shape, jnp.int32)
    for i, g in enumerate(BCAST_GROUPS):
        bc_grp = jnp.where(bc_blk == i, g, bc_grp)
    sel_r = lax.broadcasted_iota(jnp.int32, (LANES, LANES), 0)
    for idx in range(GDN_CHAINS):
        pair = (idx % 2) * DN_HEADS + hg * GDN_HG + idx // 2
        hi, lo = G_GHI * NPAIR + pair, G_GLO * NPAIR + pair
        esel_s[idx, :, :N_BC * LANES] = jnp.where(
            bc_row == bc_grp * NPAIR + pair, 1.0, 0.0).astype(BF16)
        esel_s[idx, :, N_BC * LANES:] = jnp.where(
            (sel_r == hi) | (sel_r == lo), 1.0, 0.0).astype(BF16)
        st_s[idx] = jnp.zeros((HEAD_DIM, HEAD_DIM), F32)

    ri = lax.broadcasted_iota(jnp.int32, (CHUNK, CHUNK), 0)
    ci = lax.broadcasted_iota(jnp.int32, (CHUNK, CHUNK), 1)

    bf = lambda x: x.astype(BF16)
    tri = ((ri >= ci, ri > ci), (ri <= ci, ri < ci))

    nblock = nchunk // GDN_PREP_G
    chains = [(g, hh, d) for g in range(GDN_PREP_G) for hh in range(GDN_HG) for d in range(2)]
    chain_idx = [2 * hh + d for _, hh, d in chains]
    chain_dir = [d for _, _, d in chains]

    def prep_stages(i):
        step = [i * GDN_PREP_G + g for g, _, _ in chains]
        chunk = [s if d == 0 else nchunk - 1 - s for s, d in zip(step, chain_dir)]
        rows = [pl.ds(pl.multiple_of(n * CHUNK, CHUNK), CHUNK) for n in chunk]
        kc = [ks[r, heads[hh]] for r, (_, hh, _) in zip(rows, chains)]
        qc = [qs[r, heads[hh]] for r, (_, hh, _) in zip(rows, chains)]
        vf = [vs[r, heads[hh]].astype(F32) for r, (_, hh, _) in zip(rows, chains)]
        kf = _par(lambda x: x.astype(F32), kc)
        qf = _par(lambda x: x.astype(F32), qc)
        span = GDN_PREP_G * CHUNK
        bcg = []
        for idx in range(GDN_CHAINS):
            first = i * GDN_PREP_G if idx % 2 == 0 else nchunk - (i + 1) * GDN_PREP_G
            blk_rows = pl.ds(pl.multiple_of(first * CHUNK, span), span)
            bcg.append(_nn(cols_ref[0, blk_rows, :], esel_s[idx]))
        local = [(g if d == 0 else GDN_PREP_G - 1 - g) * CHUNK for g, _, d in chains]
        bc = [bcg[idx][r0:r0 + CHUNK, :] for idx, r0 in zip(chain_idx, local)]
        qkk = _par(lambda k_, q_: _nt(jnp.concatenate([k_, q_], axis=0), k_), kc, qc)
        yield
        beta_b, g_bc = (
            [x[:, i_ * LANES:(i_ + 1) * LANES] for x in bc] for i_ in range(N_BC + 1))
        g_col = [x[:, :CHUNK] for x in g_bc]
        g_row = [x.T[:CHUNK, :] for x in g_bc]
        eg_b = _par(jnp.exp, g_bc)
        last = [slice(CHUNK - 1, CHUNK) if d == 0 else slice(0, 1) for d in chain_dir]
        el = [jnp.broadcast_to(eg[r, :], (SUBLANES, LANES)) for r, eg in zip(last, eg_b)]
        ed_t = [jnp.broadcast_to(jnp.exp(gc_[r, :] - gr_[:1, :]), (HEAD_DIM, CHUNK))
                for r, gc_, gr_ in zip(last, g_col, g_row)]
        decay = [jnp.exp(jnp.where(tri[d][0], gc_ - gr_, 0.0))
                 for d, gc_, gr_ in zip(chain_dir, g_col, g_row)]
        a = [jnp.where(tri[d][1], x[:CHUNK] * be[:, :CHUNK] * dc, 0.0)
             for d, x, be, dc in zip(chain_dir, qkk, beta_b, decay)]
        tmat = yield from _unit_triangular_inverses(a)
        rhs = [jnp.concatenate([bf(k_ * (be * eg)), bf(v_ * be)], axis=1)
               for k_, v_, be, eg in zip(kf, vf, beta_b, eg_b)]
        wu = _par(lambda t_, r_: bf(_nn(bf(t_), r_)), tmat, rhs)
        yield
        kdt = [bf(k_.T * ed) for k_, ed in zip(kf, ed_t)]
        qkm = [bf(jnp.where(tri[d][0], x[CHUNK:] * dc, 0.0))
               for d, x, dc in zip(chain_dir, qkk, decay)]
        res = _par(lambda kt, qm, w_: _nn(jnp.concatenate([kt, qm], axis=0), w_),
                   kdt, qkm, wu)
        yield
        for idx, n, rs, q_, eg, el_ in zip(chain_idx, chunk, res, qf, eg_b, el):
            mq_s[idx, n, :HEAD_DIM] = bf(-rs[:HEAD_DIM, :HEAD_DIM])
            mq_s[idx, n, HEAD_DIM:] = bf(q_ * eg - rs[HEAD_DIM:, :HEAD_DIM])
            co_s[idx, n] = rs[:, HEAD_DIM:]
            el_s[idx, n] = el_

    def scan_step(s):
        for idx in range(GDN_CHAINS):
            n = s if idx % 2 == 0 else nchunk - 1 - s
            st = st_s[idx]
            x = co_s[idx, n] + _nn(mq_s[idx, n], st.astype(BF16))
            oacc[idx, pl.ds(pl.multiple_of(n * CHUNK, CHUNK), CHUNK), :] = x[HEAD_DIM:]
            el = jnp.broadcast_to(el_s[idx, n][:1, :], (HEAD_DIM, HEAD_DIM))
            st_s[idx] = st * el + x[:HEAD_DIM]

    def first_block(i, carry):
        _run_staged(prep_stages(i), [])
        return carry

    def block(i, carry):
        steps = [functools.partial(scan_step, (i - 1) * GDN_PREP_G + g)
                 for g in range(GDN_PREP_G)]
        _run_staged(prep_stages(i), steps)
        return carry

    def last_steps(s, carry):
        scan_step(s)
        return carry

    lax.fori_loop(0, 1, first_block, 0)
    lax.fori_loop(1, nblock, block, 0)
    lax.fori_loop(nchunk - GDN_PREP_G, nchunk, last_steps, 0)

    for hh, cs in enumerate(heads):
        o = oacc[2 * hh] + oacc[2 * hh + 1]
        o = o * lax.rsqrt(jnp.mean(o * o, axis=-1, keepdims=True) + RMS_EPS) * nw_ref[...]
        o_ref[0, :, cs] = (o * _silu(z_ref[0, :, cs].astype(F32))).astype(BF16)


def _gdn(proj3d, cols, conv_w, norm_w):
    b, l, _ = proj3d.shape
    nchunk = l // CHUNK
    assert DN_HEADS % GDN_HG == 0 and nchunk % GDN_PREP_G == 0
    groups = DN_HEADS // GDN_HG
    blk = lambda off: pl.BlockSpec((1, l, GDN_W), lambda i, h, off=off: (i, 0, off // GDN_W + h))
    wblk = lambda off: pl.BlockSpec((CONV_K, GDN_W), lambda i, h, off=off: (0, off // GDN_W + h))
    return pl.pallas_call(
        _gdn_kernel,
        grid=(b, groups),
        in_specs=[
            blk(OFF_DN), blk(OFF_DN + DN_WIDTH), blk(OFF_DN + 2 * DN_WIDTH), blk(OFF_Z),
            pl.BlockSpec((1, l, LANES), lambda i, h: (i, 0, 0)),
            wblk(0), wblk(DN_WIDTH), wblk(2 * DN_WIDTH),
            pl.BlockSpec((1, HEAD_DIM), lambda i, h: (0, 0)),
        ],
        out_specs=pl.BlockSpec((1, l, GDN_W), lambda i, h: (i, 0, h)),
        out_shape=jax.ShapeDtypeStruct((b, l, DN_WIDTH), BF16),
        scratch_shapes=[
            pltpu.VMEM((l + 2 * CONV_PAD, HEAD_DIM), F32),
            pltpu.VMEM((l, GDN_W), BF16),
            pltpu.VMEM((l, GDN_W), BF16),
            pltpu.VMEM((l, GDN_W), BF16),
            pltpu.VMEM((GDN_CHAINS, LANES, (N_BC + 1) * LANES), BF16),
            pltpu.VMEM((GDN_CHAINS, nchunk, MQ_ROWS, HEAD_DIM), BF16),
            pltpu.VMEM((GDN_CHAINS, nchunk, MQ_ROWS, HEAD_DIM), F32),
            pltpu.VMEM((GDN_CHAINS, nchunk, SUBLANES, HEAD_DIM), F32),
            pltpu.VMEM((GDN_CHAINS, HEAD_DIM, HEAD_DIM), F32),
            pltpu.VMEM((GDN_CHAINS, l, HEAD_DIM), F32),
        ],
        compiler_params=pltpu.CompilerParams(
            dimension_semantics=("arbitrary", "arbitrary"),
            vmem_limit_bytes=VMEM_LIMIT),
        name="gdn",
    )(proj3d, proj3d, proj3d, proj3d, cols, conv_w, conv_w, conv_w, norm_w)


def _out_proj_kernel(x_ref, yna_ref, ydn_ref, wa_ref, wb_ref, nw_ref, h_ref, hn_ref):
    mixed = _nn(yna_ref[...], wa_ref[...]) + _nn(ydn_ref[...], wb_ref[...])
    hres = x_ref[...] + mixed
    h_ref[...] = hres
    ms = jnp.mean(hres * hres, axis=-1, keepdims=True)
    hn_ref[...] = (hres * lax.rsqrt(ms + RMS_EPS) * nw_ref[...]).astype(BF16)


def _out_proj(x2d, yna, ydn, w_a, w_b, norm_w, tm=512):
    t = x2d.shape[0]
    row = lambda w: pl.BlockSpec((tm, w), lambda i: (i, 0))
    full = lambda a: pl.BlockSpec(a.shape, lambda i: (0, 0))
    return pl.pallas_call(
        _out_proj_kernel,
        grid=(t // tm,),
        in_specs=[row(D_MODEL), row(NA_WIDTH), row(DN_WIDTH), full(w_a), full(w_b),
                  full(norm_w)],
        out_specs=[row(D_MODEL), row(D_MODEL)],
        out_shape=[jax.ShapeDtypeStruct((t, D_MODEL), F32),
                   jax.ShapeDtypeStruct((t, D_MODEL), BF16)],
        compiler_params=pltpu.CompilerParams(
            dimension_semantics=("arbitrary",), vmem_limit_bytes=VMEM_LIMIT),
        name="out_proj",
    )(x2d, yna, ydn, w_a, w_b, norm_w)


def _ffn_kernel(hn_ref, h_ref, wg_ref, wu_ref, wd_ref, nw_ref, o_ref, acc_ref):
    f = pl.program_id(1)

    @pl.when(f == 0)
    def _():
        acc_ref[...] = h_ref[...]

    hn = hn_ref[...]
    g = _nn(hn, wg_ref[...])
    u = _nn(hn, wu_ref[...])
    acc_ref[...] += _nn((_silu(g) * u).astype(BF16), wd_ref[...])

    @pl.when(f == pl.num_programs(1) - 1)
    def _():
        y = acc_ref[...]
        ms = jnp.mean(y * y, axis=-1, keepdims=True)
        o_ref[...] = y * lax.rsqrt(ms + RMS_EPS) * nw_ref[...]


def _ffn(hn, hres, w_gate, w_up, w_down, norm_w, tm=512, tf=512):
    t = hn.shape[0]
    return pl.pallas_call(
        _ffn_kernel,
        grid=(t // tm, D_FF // tf),
        in_specs=[
            pl.BlockSpec((tm, D_MODEL), lambda i, f: (i, 0)),
            pl.BlockSpec((tm, D_MODEL), lambda i, f: (i, 0)),
            pl.BlockSpec((D_MODEL, tf), lambda i, f: (0, f)),
            pl.BlockSpec((D_MODEL, tf), lambda i, f: (0, f)),
            pl.BlockSpec((tf, D_MODEL), lambda i, f: (f, 0)),
            pl.BlockSpec((1, D_MODEL), lambda i, f: (0, 0)),
        ],
        out_specs=pl.BlockSpec((tm, D_MODEL), lambda i, f: (i, 0)),
        out_shape=jax.ShapeDtypeStruct((t, D_MODEL), F32),
        scratch_shapes=[pltpu.VMEM((tm, D_MODEL), F32)],
        compiler_params=pltpu.CompilerParams(
            dimension_semantics=("arbitrary", "arbitrary"),
            vmem_limit_bytes=VMEM_LIMIT),
        name="ffn",
    )(hn, hres, w_gate, w_up, w_down, norm_w)


def _lane_table(per_pair):
    return jnp.tile(per_pair.astype(F32), LANES // NPAIR).reshape(1, LANES)


def _trunk(x, p):
    b, l, _ = x.shape
    x2d = x.reshape(b * l, D_MODEL)
    proj, ba = _in_proj(x2d, p["norm_mix_w"], p["w_main"], p["w_ba"])
    proj3d = proj.reshape(b, l, MAIN_COLS)
    cols = _gates(ba.reshape(b, l, LANES), p["alog_lanes"], p["dtb_lanes"])
    y_na = _na(proj3d, p["rpb_flat"])
    y_dn = _gdn(proj3d, cols, p["conv_w"], p["dn_norm_w"])
    hres, hn = _out_proj(x2d, y_na.reshape(b * l, NA_WIDTH), y_dn.reshape(b * l, DN_WIDTH),
                         p["w_out_na"], p["w_out_dn"], p["norm_ffn_w"])
    y = _ffn(hn, hres, p["w_gate"], p["w_up"], p["w_down"], p["final_norm_w"])
    return y.reshape(b, l, D_MODEL)


def kernel(x_prompt, x_sample, norm_mix_w, w_in, na_rpb, dn_conv_w, dn_A_log, dn_dt_bias,
           dn_norm_w, w_out, norm_ffn_w, w_gate, w_up, w_down, final_norm_w):
    w_in0 = w_in[0]
    w_b = w_in0[:, OFF_B:OFF_A]
    w_a = w_in0[:, OFF_A:OFF_A + NPAIR]
    pad = jnp.zeros((D_MODEL, LANES - N_GATE_GROUPS * NPAIR), F32)
    p = {
        "norm_mix_w": norm_mix_w[0].reshape(1, D_MODEL),
        "w_main": w_in0.astype(BF16),
        "w_ba": jnp.concatenate([w_b] + [w_a] * (N_GATE_GROUPS - 1) + [pad],
                                axis=1).astype(BF16),
        "alog_lanes": _lane_table(dn_A_log[0].reshape(NPAIR)),
        "dtb_lanes": _lane_table(dn_dt_bias[0].reshape(NPAIR)),
        "rpb_flat": na_rpb[0].reshape(-1),
        "conv_w": dn_conv_w[0],
        "dn_norm_w": dn_norm_w[0].reshape(1, HEAD_DIM),
        "w_out_na": w_out[0, :NA_WIDTH].astype(BF16),
        "w_out_dn": w_out[0, NA_WIDTH:].astype(BF16),
        "norm_ffn_w": norm_ffn_w[0].reshape(1, D_MODEL),
        "w_gate": w_gate[0].astype(BF16),
        "w_up": w_up[0].astype(BF16),
        "w_down": w_down[0].astype(BF16),
        "final_norm_w": final_norm_w.reshape(1, D_MODEL),
    }
    return (_trunk(x_prompt, p), _trunk(x_sample, p))
```

```python
import functools

import jax
import jax.numpy as jnp
from jax import lax
from jax.experimental import pallas as pl
from jax.experimental.pallas import tpu as pltpu

D_MODEL = 2048
GRID_W = 64
HEAD_DIM = 128
NA_HEADS = 8
DN_HEADS = 8
NA_WIDTH = NA_HEADS * HEAD_DIM
DN_WIDTH = DN_HEADS * HEAD_DIM
NA_ROWS = 8
NA_COLS = 16
CONV_K = 5
CHUNK = 64
D_FF = 5632
RMS_EPS = 1e-6
L2_EPS = 1e-6

OFF_NA = 0
OFF_DN = OFF_NA + 3 * NA_WIDTH
OFF_Z = OFF_DN + 3 * DN_WIDTH
OFF_B = OFF_Z + DN_WIDTH
OFF_A = OFF_B + 2 * DN_HEADS
MAIN_COLS = OFF_B

LANES = 128
SUBLANES = 8
NPAIR = 2 * DN_HEADS
G_BETA, G_GHI, G_GLO = range(3)
N_GATE_GROUPS = 3
MASK_NEG = -1e30
VMEM_LIMIT = 60 * 1024 * 1024

F32 = jnp.float32
BF16 = jnp.bfloat16


def _silu(x):
    return x * (1.0 / (1.0 + jnp.exp(-x)))


def _nt(a, b):
    return lax.dot_general(a, b, (((1,), (1,)), ((), ())), preferred_element_type=F32)


def _nn(a, b):
    return jnp.dot(a, b, preferred_element_type=F32)


def _in_proj_kernel(x_ref, nw_ref, w_ref, wba_ref, proj_ref, ba_ref, xn_ref):
    @pl.when(pl.program_id(1) == 0)
    def _():
        x = x_ref[...]
        ms = jnp.mean(x * x, axis=-1, keepdims=True)
        xn = (x * lax.rsqrt(ms + RMS_EPS) * nw_ref[...]).astype(BF16)
        xn_ref[...] = xn
        ba_ref[...] = _nn(xn, wba_ref[...])

    proj_ref[...] = _nn(xn_ref[...], w_ref[...]).astype(BF16)


def _in_proj(x2d, norm_w, w_main, w_ba, tm=1024, tn=1792):
    t = x2d.shape[0]
    return pl.pallas_call(
        _in_proj_kernel,
        grid=(t // tm, MAIN_COLS // tn),
        in_specs=[
            pl.BlockSpec((tm, D_MODEL), lambda i, j: (i, 0)),
            pl.BlockSpec((1, D_MODEL), lambda i, j: (0, 0)),
            pl.BlockSpec((D_MODEL, tn), lambda i, j: (0, j)),
            pl.BlockSpec((D_MODEL, LANES), lambda i, j: (0, 0)),
        ],
        out_specs=[
            pl.BlockSpec((tm, tn), lambda i, j: (i, j)),
            pl.BlockSpec((tm, LANES), lambda i, j: (i, 0)),
        ],
        out_shape=[
            jax.ShapeDtypeStruct((t, MAIN_COLS), BF16),
            jax.ShapeDtypeStruct((t, LANES), F32),
        ],
        scratch_shapes=[pltpu.VMEM((tm, D_MODEL), BF16)],
        compiler_params=pltpu.CompilerParams(
            dimension_semantics=("arbitrary", "arbitrary"),
            vmem_limit_bytes=VMEM_LIMIT),
        name="in_proj",
    )(x2d, norm_w, w_main, w_ba)


def _gates_kernel(ba_ref, alog_ref, dtb_ref, out_ref):
    x = ba_ref[0]
    n = x.shape[0]
    lane = lax.broadcasted_iota(jnp.int32, x.shape, 1)
    pos = lax.broadcasted_iota(jnp.int32, x.shape, 0) % CHUNK
    grp = lane // NPAIR
    is_bwd = (lane % NPAIR) >= DN_HEADS

    beta = 1.0 / (1.0 + jnp.exp(-x))
    a = x + dtb_ref[...]
    softplus = jnp.maximum(a, 0.0) + jnp.log(1.0 + jnp.exp(-jnp.abs(a)))
    g = -jnp.exp(alog_ref[...]) * softplus

    pre = g
    suf = g
    s = 1
    while s < CHUNK:
        pre = pre + jnp.where(pos >= s, pltpu.roll(pre, s, axis=0), 0.0)
        suf = suf + jnp.where(pos < CHUNK - s, pltpu.roll(suf, n - s, axis=0), 0.0)
        s *= 2
    gc = jnp.where(is_bwd, suf, pre)
    ghi = gc.astype(BF16).astype(F32)

    out = jnp.where(grp == G_BETA, beta, 0.0)
    out = jnp.where(grp == G_GHI, ghi, out)
    out = jnp.where(grp == G_GLO, gc - ghi, out)
    out_ref[0] = out.astype(BF16)


def _gates(ba3d, alog_lanes, dtb_lanes):
    b, l, _ = ba3d.shape
    return pl.pallas_call(
        _gates_kernel,
        grid=(b,),
        in_specs=[
            pl.BlockSpec((1, l, LANES), lambda i: (i, 0, 0)),
            pl.BlockSpec((1, LANES), lambda i: (0, 0)),
            pl.BlockSpec((1, LANES), lambda i: (0, 0)),
        ],
        out_specs=pl.BlockSpec((1, l, LANES), lambda i: (i, 0, 0)),
        out_shape=jax.ShapeDtypeStruct((b, l, LANES), BF16),
        compiler_params=pltpu.CompilerParams(
            dimension_semantics=("arbitrary",), vmem_limit_bytes=VMEM_LIMIT),
        name="gates",
    )(ba3d, alog_lanes, dtb_lanes)


N_DR = 2 * NA_ROWS - 1
N_DC = 2 * NA_COLS - 1
N_TAB = N_DR - 1
NA_ROW_G = 32


def _na_kernel(rpb_ref, q_ref, k_ref, v_ref, o_ref, tab_ref):
    h = pl.program_id(0)
    rows = q_ref.shape[1] // GRID_W
    band = NA_ROWS * GRID_W

    @pl.when(pl.program_id(1) == 0)
    def _():
        shape = (GRID_W, 2 * GRID_W)
        c = lax.broadcasted_iota(jnp.int32, shape, 0)
        lane = lax.broadcasted_iota(jnp.int32, shape, 1)
        kc = lane % GRID_W
        upper = lane >= GRID_W
        start = jnp.clip(c - NA_COLS // 2, 0, GRID_W - NA_COLS)
        valid = (kc >= start) & (kc < start + NA_COLS)
        delta = kc - c + (NA_COLS - 1)

        def build(d, carry):
            base = (h * N_DR + d) * N_DC
            val = jnp.full(shape, MASK_NEG, F32)
            for dc in range(N_DC):
                s0 = rpb_ref[base + dc]
                s1 = rpb_ref[base + N_DC + dc]
                val = jnp.where(delta == dc, jnp.where(upper, s1, s0), val)
            tab_ref[d] = jnp.where(valid, val, MASK_NEG)
            return carry

        lax.fori_loop(0, N_TAB, build, 0)

    scale = HEAD_DIM ** -0.5

    nparts = band // LANES

    def row_group(i, carry):
        rr = [i * NA_ROW_G + g for g in range(NA_ROW_G)]
        rs = [jnp.clip(r - NA_ROWS // 2, 0, rows - NA_ROWS) for r in rr]
        d0 = [s_ - r + (NA_ROWS - 1) for s_, r in zip(rs, rr)]
        qrow = [pl.ds(pl.multiple_of(r * GRID_W, GRID_W), GRID_W) for r in rr]
        brow = [pl.ds(pl.multiple_of(s_ * GRID_W, GRID_W), band) for s_ in rs]
        s = [_nt(q_ref[0, qr, :], k_ref[0, br, :]) for qr, br in zip(qrow, brow)]
        parts = [[x[:, j * LANES:(j + 1) * LANES] * scale + tab_ref[d + 2 * j]
                  for j in range(nparts)] for x, d in zip(s, d0)]
        m = [functools.reduce(jnp.maximum, ps_).max(axis=-1, keepdims=True) for ps_ in parts]
        e = [[jnp.exp(p_ - m_) for p_ in ps_] for ps_, m_ in zip(parts, m)]
        l = [functools.reduce(jnp.add, es_).sum(axis=-1, keepdims=True) for es_ in e]
        o = [_nn(jnp.concatenate(es_, axis=1).astype(BF16), v_ref[0, br, :])
             for es_, br in zip(e, brow)]
        for qr, o_, l_ in zip(qrow, o, l):
            o_ref[0, qr, :] = (o_ / l_).astype(BF16)
        return carry

    lax.fori_loop(0, rows // NA_ROW_G, row_group, 0)


def _na(proj3d, rpb_flat):
    b, l, _ = proj3d.shape
    blk = lambda off: pl.BlockSpec((1, l, HEAD_DIM), lambda h, i, off=off: (i, 0, off + h))
    return pl.pallas_call(
        _na_kernel,
        grid=(NA_HEADS, b),
        in_specs=[
            pl.BlockSpec(memory_space=pltpu.SMEM),
            blk(0), blk(NA_HEADS), blk(2 * NA_HEADS),
        ],
        out_specs=pl.BlockSpec((1, l, HEAD_DIM), lambda h, i: (i, 0, h)),
        out_shape=jax.ShapeDtypeStruct((b, l, NA_WIDTH), BF16),
        scratch_shapes=[pltpu.VMEM((N_TAB, GRID_W, 2 * GRID_W), F32)],
        compiler_params=pltpu.CompilerParams(
            dimension_semantics=("arbitrary", "arbitrary"),
            vmem_limit_bytes=VMEM_LIMIT),
        name="na",
    )(rpb_flat, proj3d, proj3d, proj3d)


SUB = 16


CONV_PAD = SUBLANES


CONV_TILE = 256


def _conv_silu(x_ref, cs, w, pad_ref, finish):
    n = x_ref.shape[1]
    half = CONV_K // 2

    def fill(t, carry):
        r0 = pl.multiple_of(t * CONV_TILE, CONV_TILE)
        pad_ref[pl.ds(CONV_PAD + r0, CONV_TILE), :] = (
            x_ref[0, pl.ds(r0, CONV_TILE), cs].astype(F32))
        return carry

    def tile(t, carry):
        r0 = pl.multiple_of(t * CONV_TILE, CONV_TILE)
        acc = None
        for i in range(CONV_K):
            term = pad_ref[pl.ds(r0 + (CONV_PAD + i - half), CONV_TILE), :] * w[i:i + 1, :]
            acc = term if acc is None else acc + term
        finish(r0, _silu(acc))
        return carry

    lax.fori_loop(0, n // CONV_TILE, fill, 0)
    lax.fori_loop(0, n // CONV_TILE, tile, 0, unroll=4)


def _l2norm(x):
    return x * lax.rsqrt(jnp.sum(x * x, axis=-1, keepdims=True) + L2_EPS)


def _par(f, *lists):
    return [f(*xs) for xs in zip(*lists)]


def _unit_triangular_inverses(mats):
    n = mats[0].shape[0]
    i = lax.broadcasted_iota(jnp.int32, (n, n), 0)
    j = lax.broadcasted_iota(jnp.int32, (n, n), 1)
    same = (i // SUB) == (j // SUB)
    eye = jnp.where(i == j, 1.0, 0.0)
    bf = lambda x: x.astype(BF16)
    d = _par(lambda a: jnp.where(same, a, 0.0), mats)
    rest = _par(lambda a, dd: bf(a - dd), mats, d)
    dp = _par(bf, d)
    p = _par(lambda dd: eye - dd, d)
    step = 1
    while 2 * step < SUB:
        dp = _par(lambda x: bf(_nn(x, x)), dp)
        yield
        p = _par(lambda pp, x: pp + _nn(bf(pp), x), p, dp)
        yield
        step *= 2
    m = _par(lambda pp, rr: _nn(bf(pp), rr), p, rest)
    yield
    m2 = _par(lambda mm: bf(_nn(bf(mm), bf(mm))), m)
    yield
    r = _par(lambda mm, sq: _nn(bf(eye - mm), sq) - mm, m, m2)
    yield
    return _par(lambda pp, rr: pp + _nn(bf(rr), bf(pp)), p, r)


def _run_staged(stages, extras):
    extras = list(extras)
    for k, _ in enumerate(stages):
        if extras and k % 2 == 0:
            extras.pop(0)()
    for f in extras:
        f()


GDN_HG = 2
GDN_PREP_G = 8
GDN_CHAINS = 2 * GDN_HG
GDN_W = GDN_HG * HEAD_DIM
MQ_ROWS = HEAD_DIM + CHUNK
BCAST_GROUPS = (G_BETA,)
N_BC = len(BCAST_GROUPS)


def _gdn_kernel(q_ref, k_ref, v_ref, z_ref, cols_ref, wq_ref, wk_ref, wv_ref, nw_ref,
                o_ref, pad_s, qs, ks, vs, esel_s, mq_s, co_s, el_s, st_s, oacc):
    hg = pl.program_id(1)
    l = q_ref.shape[1]
    nchunk = l // CHUNK
    heads = [slice(hh * HEAD_DIM, (hh + 1) * HEAD_DIM) for hh in range(GDN_HG)]

    zero_pad = jnp.zeros((CONV_PAD, HEAD_DIM), F32)
    pad_s[:CONV_PAD, :] = zero_pad
    pad_s[CONV_PAD + l:, :] = zero_pad
    for cs in heads:
        def put_q(r0, y, cs=cs):
            qs[pl.ds(r0, CONV_TILE), cs] = (_l2norm(y) * (HEAD_DIM ** -0.5)).astype(BF16)

        def put_k(r0, y, cs=cs):
            ks[pl.ds(r0, CONV_TILE), cs] = _l2norm(y).astype(BF16)

        def put_v(r0, y, cs=cs):
            vs[pl.ds(r0, CONV_TILE), cs] = y.astype(BF16)

        _conv_silu(q_ref, cs, wq_ref[:, cs], pad_s, put_q)
        _conv_silu(k_ref, cs, wk_ref[:, cs], pad_s, put_k)
        _conv_silu(v_ref, cs, wv_ref[:, cs], pad_s, put_v)

    bc_shape = (LANES, N_BC * LANES)
    bc_row = lax.broadcasted_iota(jnp.int32, bc_shape, 0)
    bc_blk = lax.broadcasted_iota(jnp.int32, bc_shape, 1) // LANES
    bc_grp = jnp.zeros(bc_shape, jnp.int32)
    for i, g in enumerate(BCAST_GROUPS):
        bc_grp = jnp.where(bc_blk == i, g, bc_grp)
    sel_r = lax.broadcasted_iota(jnp.int32, (LANES, LANES), 0)
    for idx in range(GDN_CHAINS):
        pair = (idx % 2) * DN_HEADS + hg * GDN_HG + idx // 2
        hi, lo = G_GHI * NPAIR + pair, G_GLO * NPAIR + pair
        esel_s[idx, :, :N_BC * LANES] = jnp.where(
            bc_row == bc_grp * NPAIR + pair, 1.0, 0.0).astype(BF16)
        esel_s[idx, :, N_BC * LANES:] = jnp.where(
            (sel_r == hi) | (sel_r == lo), 1.0, 0.0).astype(BF16)
        st_s[idx] = jnp.zeros((HEAD_DIM, HEAD_DIM), F32)

    ri = lax.broadcasted_iota(jnp.int32, (CHUNK, CHUNK), 0)
    ci = lax.broadcasted_iota(jnp.int32, (CHUNK, CHUNK), 1)

    bf = lambda x: x.astype(BF16)
    tri = ((ri >= ci, ri > ci), (ri <= ci, ri < ci))

    nblock = nchunk // GDN_PREP_G
    chains = [(g, hh, d) for g in range(GDN_PREP_G) for hh in range(GDN_HG) for d in range(2)]
    chain_idx = [2 * hh + d for _, hh, d in chains]
    chain_dir = [d for _, _, d in chains]

    def prep_stages(i):
        step = [i * GDN_PREP_G + g for g, _, _ in chains]
        chunk = [s if d == 0 else nchunk - 1 - s for s, d in zip(step, chain_dir)]
        rows = [pl.ds(pl.multiple_of(n * CHUNK, CHUNK), CHUNK) for n in chunk]
        kc = [ks[r, heads[hh]] for r, (_, hh, _) in zip(rows, chains)]
        qc = [qs[r, heads[hh]] for r, (_, hh, _) in zip(rows, chains)]
        vf = [vs[r, heads[hh]].astype(F32) for r, (_, hh, _) in zip(rows, chains)]
        kf = _par(lambda x: x.astype(F32), kc)
        qf = _par(lambda x: x.astype(F32), qc)
        span = GDN_PREP_G * CHUNK
        bcg = []
        for idx in range(GDN_CHAINS):
            first = i * GDN_PREP_G if idx % 2 == 0 else nchunk - (i + 1) * GDN_PREP_G
            blk_rows = pl.ds(pl.multiple_of(first * CHUNK, span), span)
            bcg.append(_nn(cols_ref[0, blk_rows, :], esel_s[idx]))
        local = [(g if d == 0 else GDN_PREP_G - 1 - g) * CHUNK for g, _, d in chains]
        bc = [bcg[idx][r0:r0 + CHUNK, :] for idx, r0 in zip(chain_idx, local)]
        qkk = _par(lambda k_, q_: _nt(jnp.concatenate([k_, q_], axis=0), k_), kc, qc)
        yield
        beta_b, g_bc = (
            [x[:, i_ * LANES:(i_ + 1) * LANES] for x in bc] for i_ in range(N_BC + 1))
        g_col = [x[:, :CHUNK] for x in g_bc]
        g_row = [x.T[:CHUNK, :] for x in g_bc]
        eg_b = _par(jnp.exp, g_bc)
        last = [slice(CHUNK - 1, CHUNK) if d == 0 else slice(0, 1) for d in chain_dir]
        el = [jnp.broadcast_to(eg[r, :], (SUBLANES, LANES)) for r, eg in zip(last, eg_b)]
        ed_t = [jnp.broadcast_to(jnp.exp(gc_[r, :] - gr_[:1, :]), (HEAD_DIM, CHUNK))
                for r, gc_, gr_ in zip(last, g_col, g_row)]
        decay = [jnp.exp(jnp.where(tri[d][0], gc_ - gr_, 0.0))
                 for d, gc_, gr_ in zip(chain_dir, g_col, g_row)]
        a = [jnp.where(tri[d][1], x[:CHUNK] * be[:, :CHUNK] * dc, 0.0)
             for d, x, be, dc in zip(chain_dir, qkk, beta_b, decay)]
        tmat = yield from _unit_triangular_inverses(a)
        rhs = [jnp.concatenate([bf(k_ * (be * eg)), bf(v_ * be)], axis=1)
               for k_, v_, be, eg in zip(kf, vf, beta_b, eg_b)]
        wu = _par(lambda t_, r_: bf(_nn(bf(t_), r_)), tmat, rhs)
        yield
        kdt = [bf(k_.T * ed) for k_, ed in zip(kf, ed_t)]
        qkm = [bf(jnp.where(tri[d][0], x[CHUNK:] * dc, 0.0))
               for d, x, dc in zip(chain_dir, qkk, decay)]
        res = _par(lambda kt, qm, w_: _nn(jnp.concatenate([kt, qm], axis=0), w_),
                   kdt, qkm, wu)
        yield
        for idx, n, rs, q_, eg, el_ in zip(chain_idx, chunk, res, qf, eg_b, el):
            mq_s[idx, n, :HEAD_DIM] = bf(-rs[:HEAD_DIM, :HEAD_DIM])
            mq_s[idx, n, HEAD_DIM:] = bf(q_ * eg - rs[HEAD_DIM:, :HEAD_DIM])
            co_s[idx, n] = rs[:, HEAD_DIM:]
            el_s[idx, n] = el_

    def scan_step(s):
        for idx in range(GDN_CHAINS):
            n = s if idx % 2 == 0 else nchunk - 1 - s
            st = st_s[idx]
            x = co_s[idx, n] + _nn(mq_s[idx, n], st.astype(BF16))
            oacc[idx, pl.ds(pl.multiple_of(n * CHUNK, CHUNK), CHUNK), :] = x[HEAD_DIM:]
            el = jnp.broadcast_to(el_s[idx, n][:1, :], (HEAD_DIM, HEAD_DIM))
            st_s[idx] = st * el + x[:HEAD_DIM]

    def first_block(i, carry):
        _run_staged(prep_stages(i), [])
        return carry

    def block(i, carry):
        steps = [functools.partial(scan_step, (i - 1) * GDN_PREP_G + g)
                 for g in range(GDN_PREP_G)]
        _run_staged(prep_stages(i), steps)
        return carry

    def last_steps(s, carry):
        scan_step(s)
        return carry

    lax.fori_loop(0, 1, first_block, 0)
    lax.fori_loop(1, nblock, block, 0)
    lax.fori_loop(nchunk - GDN_PREP_G, nchunk, last_steps, 0)

    for hh, cs in enumerate(heads):
        o = oacc[2 * hh] + oacc[2 * hh + 1]
        o = o * lax.rsqrt(jnp.mean(o * o, axis=-1, keepdims=True) + RMS_EPS) * nw_ref[...]
        o_ref[0, :, cs] = (o * _silu(z_ref[0, :, cs].astype(F32))).astype(BF16)


def _gdn(proj3d, cols, conv_w, norm_w):
    b, l, _ = proj3d.shape
    nchunk = l // CHUNK
    assert DN_HEADS % GDN_HG == 0 and nchunk % GDN_PREP_G == 0
    groups = DN_HEADS // GDN_HG
    blk = lambda off: pl.BlockSpec((1, l, GDN_W), lambda i, h, off=off: (i, 0, off // GDN_W + h))
    wblk = lambda off: pl.BlockSpec((CONV_K, GDN_W), lambda i, h, off=off: (0, off // GDN_W + h))
    return pl.pallas_call(
        _gdn_kernel,
        grid=(b, groups),
        in_specs=[
            blk(OFF_DN), blk(OFF_DN + DN_WIDTH), blk(OFF_DN + 2 * DN_WIDTH), blk(OFF_Z),
            pl.BlockSpec((1, l, LANES), lambda i, h: (i, 0, 0)),
            wblk(0), wblk(DN_WIDTH), wblk(2 * DN_WIDTH),
            pl.BlockSpec((1, HEAD_DIM), lambda i, h: (0, 0)),
        ],
        out_specs=pl.BlockSpec((1, l, GDN_W), lambda i, h: (i, 0, h)),
        out_shape=jax.ShapeDtypeStruct((b, l, DN_WIDTH), BF16),
        scratch_shapes=[
            pltpu.VMEM((l + 2 * CONV_PAD, HEAD_DIM), F32),
            pltpu.VMEM((l, GDN_W), BF16),
            pltpu.VMEM((l, GDN_W), BF16),
            pltpu.VMEM((l, GDN_W), BF16),
            pltpu.VMEM((GDN_CHAINS, LANES, (N_BC + 1) * LANES), BF16),
            pltpu.VMEM((GDN_CHAINS, nchunk, MQ_ROWS, HEAD_DIM), BF16),
            pltpu.VMEM((GDN_CHAINS, nchunk, MQ_ROWS, HEAD_DIM), F32),
            pltpu.VMEM((GDN_CHAINS, nchunk, SUBLANES, HEAD_DIM), F32),
            pltpu.VMEM((GDN_CHAINS, HEAD_DIM, HEAD_DIM), F32),
            pltpu.VMEM((GDN_CHAINS, l, HEAD_DIM), F32),
        ],
        compiler_params=pltpu.CompilerParams(
            dimension_semantics=("arbitrary", "arbitrary"),
            vmem_limit_bytes=VMEM_LIMIT),
        name="gdn",
    )(proj3d, proj3d, proj3d, proj3d, cols, conv_w, conv_w, conv_w, norm_w)


def _mix_ffn_kernel(x_ref, yna_ref, ydn_ref, wa_ref, wb_ref, nfw_ref, wg_ref, wu_ref, wd_ref,
                    nw_ref, o_ref, acc_ref, hn_ref):
    f = pl.program_id(1)

    @pl.when(f == 0)
    def _():
        mixed = _nn(yna_ref[...], wa_ref[...]) + _nn(ydn_ref[...], wb_ref[...])
        hres = x_ref[...] + mixed
        acc_ref[...] = hres
        ms = jnp.mean(hres * hres, axis=-1, keepdims=True)
        hn_ref[...] = (hres * lax.rsqrt(ms + RMS_EPS) * nfw_ref[...]).astype(BF16)

    hn = hn_ref[...]
    g = _nn(hn, wg_ref[...])
    u = _nn(hn, wu_ref[...])
    acc_ref[...] += _nn((_silu(g) * u).astype(BF16), wd_ref[...])

    @pl.when(f == pl.num_programs(1) - 1)
    def _():
        y = acc_ref[...]
        ms = jnp.mean(y * y, axis=-1, keepdims=True)
        o_ref[...] = y * lax.rsqrt(ms + RMS_EPS) * nw_ref[...]


def _mix_ffn(x2d, yna, ydn, w_a, w_b, norm_ffn_w, w_gate, w_up, w_down, norm_w, tm=512, tf=512):
    t = x2d.shape[0]
    resident = lambda a: pl.BlockSpec(a.shape, lambda i, f: (0, 0), pipeline_mode=pl.Buffered(1))
    return pl.pallas_call(
        _mix_ffn_kernel,
        grid=(t // tm, D_FF // tf),
        in_specs=[
            pl.BlockSpec((tm, D_MODEL), lambda i, f: (i, 0)),
            pl.BlockSpec((tm, NA_WIDTH), lambda i, f: (i, 0)),
            pl.BlockSpec((tm, DN_WIDTH), lambda i, f: (i, 0)),
            resident(w_a), resident(w_b),
            pl.BlockSpec((1, D_MODEL), lambda i, f: (0, 0)),
            pl.BlockSpec((D_MODEL, tf), lambda i, f: (0, f)),
            pl.BlockSpec((D_MODEL, tf), lambda i, f: (0, f)),
            pl.BlockSpec((tf, D_MODEL), lambda i, f: (f, 0)),
            pl.BlockSpec((1, D_MODEL), lambda i, f: (0, 0)),
        ],
        out_specs=pl.BlockSpec((tm, D_MODEL), lambda i, f: (i, 0)),
        out_shape=jax.ShapeDtypeStruct((t, D_MODEL), F32),
        scratch_shapes=[pltpu.VMEM((tm, D_MODEL), F32),
                        pltpu.VMEM((tm, D_MODEL), BF16)],
        compiler_params=pltpu.CompilerParams(
            dimension_semantics=("arbitrary", "arbitrary"),
            vmem_limit_bytes=VMEM_LIMIT),
        name="mix_ffn",
    )(x2d, yna, ydn, w_a, w_b, norm_ffn_w, w_gate, w_up, w_down, norm_w)


def _lane_table(per_pair):
    return jnp.tile(per_pair.astype(F32), LANES // NPAIR).reshape(1, LANES)


def _trunk(x, p):
    b, l, _ = x.shape
    x2d = x.reshape(b * l, D_MODEL)
    proj, ba = _in_proj(x2d, p["norm_mix_w"], p["w_main"], p["w_ba"])
    proj3d = proj.reshape(b, l, MAIN_COLS)
    cols = _gates(ba.reshape(b, l, LANES), p["alog_lanes"], p["dtb_lanes"])
    y_na = _na(proj3d, p["rpb_flat"])
    y_dn = _gdn(proj3d, cols, p["conv_w"], p["dn_norm_w"])
    y = _mix_ffn(x2d, y_na.reshape(b * l, NA_WIDTH), y_dn.reshape(b * l, DN_WIDTH),
                 p["w_out_na"], p["w_out_dn"], p["norm_ffn_w"],
                 p["w_gate"], p["w_up"], p["w_down"], p["final_norm_w"])
    return y.reshape(b, l, D_MODEL)


def kernel(x_prompt, x_sample, norm_mix_w, w_in, na_rpb, dn_conv_w, dn_A_log, dn_dt_bias,
           dn_norm_w, w_out, norm_ffn_w, w_gate, w_up, w_down, final_norm_w):
    w_in0 = w_in[0]
    w_b = w_in0[:, OFF_B:OFF_A]
    w_a = w_in0[:, OFF_A:OFF_A + NPAIR]
    pad = jnp.zeros((D_MODEL, LANES - N_GATE_GROUPS * NPAIR), F32)
    p = {
        "norm_mix_w": norm_mix_w[0].reshape(1, D_MODEL),
        "w_main": w_in0.astype(BF16),
        "w_ba": jnp.concatenate([w_b] + [w_a] * (N_GATE_GROUPS - 1) + [pad],
                                axis=1).astype(BF16),
        "alog_lanes": _lane_table(dn_A_log[0].reshape(NPAIR)),
        "dtb_lanes": _lane_table(dn_dt_bias[0].reshape(NPAIR)),
        "rpb_flat": na_rpb[0].reshape(-1),
        "conv_w": dn_conv_w[0],
        "dn_norm_w": dn_norm_w[0].reshape(1, HEAD_DIM),
        "w_out_na": w_out[0, :NA_WIDTH].astype(BF16),
        "w_out_dn": w_out[0, NA_WIDTH:].astype(BF16),
        "norm_ffn_w": norm_ffn_w[0].reshape(1, D_MODEL),
        "w_gate": w_gate[0].astype(BF16),
        "w_up": w_up[0].astype(BF16),
        "w_down": w_down[0].astype(BF16),
        "final_norm_w": final_norm_w.reshape(1, D_MODEL),
    }
    return (_trunk(x_prompt, p), _trunk(x_sample, p))
```
